```python
import jax, jax.numpy as jnp
from jax import lax
import numpy as np

D_MODEL = 1024
BATCH = 32
SEQ = 2048
DEPTH = 1
DEC_BATCH = 1
DEC_SEQ = 16384
PAST_LEN = 128

D_MIX = D_MODEL
HEAD_DIM = 64
N_Q_HEADS = (D_MIX // 2) // HEAD_DIM
N_KV_HEADS = 2
Q_PER_KV = N_Q_HEADS // N_KV_HEADS
D_ATTN = N_Q_HEADS * HEAD_DIM
D_KV = N_KV_HEADS * HEAD_DIM
D_LRU = D_MIX - D_ATTN
N_LRU_BLOCKS = 8
LRU_BLOCK = D_LRU // N_LRU_BLOCKS
LRU_C = 8.0
CONV_W = 4
CONV_PAD = (2, 1)
D_IN = D_ATTN + 2 * D_KV + 2 * D_LRU
D_FF = 4 * D_MODEL
GRID_W = 64
ROPE_HALF = HEAD_DIM // 2
ROPE_THETA = 10000.0
Q_BLOCK = 128
EPS = 1e-6

kernel_name = "hymba_rglru_axial_gqa_encoder"


def rms_norm(x, g):
    xf = x.astype(jnp.float32)
    var = jnp.mean(xf * xf, axis=-1, keepdims=True)
    return (xf * lax.rsqrt(var + EPS) * g.astype(jnp.float32)).astype(x.dtype)


def axial_rope_tables(seq_len):
    rows = seq_len // GRID_W
    row_ids = jnp.repeat(jnp.arange(rows), GRID_W).astype(jnp.float32)
    col_ids = jnp.tile(jnp.arange(GRID_W), rows).astype(jnp.float32)
    inv_freq = ROPE_THETA ** (-jnp.arange(0, ROPE_HALF, 2, dtype=jnp.float32) / ROPE_HALF)
    ang_r = row_ids[:, None, None] * inv_freq
    ang_c = col_ids[:, None, None] * inv_freq
    return (jnp.cos(ang_r), jnp.sin(ang_r), jnp.cos(ang_c), jnp.sin(ang_c))


def _rotate_half(x, cos, sin):
    x1, x2 = jnp.split(x, 2, axis=-1)
    return jnp.concatenate([x1 * cos - x2 * sin, x2 * cos + x1 * sin], axis=-1)


def apply_axial_rope(x, tables):
    cos_r, sin_r, cos_c, sin_c = tables
    xf = x.astype(jnp.float32)
    x_row, x_col = jnp.split(xf, 2, axis=-1)
    out = jnp.concatenate([_rotate_half(x_row, cos_r, sin_r), _rotate_half(x_col, cos_c, sin_c)], axis=-1)
    return out.astype(x.dtype)


def bidir_gqa(q, k, v):
    B, S = q.shape[0], q.shape[1]
    n_blk = S // Q_BLOCK
    qb = (q * (HEAD_DIM ** -0.5)).reshape(B, n_blk, Q_BLOCK, N_KV_HEADS, Q_PER_KV, HEAD_DIM)
    qb = qb.transpose(1, 0, 2, 3, 4, 5)

    def one_block(q_blk):
        s = jnp.einsum('bqkgd,bskd->bkgqs', q_blk, k, preferred_element_type=jnp.float32)
        p = jax.nn.softmax(s, axis=-1).astype(v.dtype)
        return jnp.einsum('bkgqs,bskd->bqkgd', p, v)

    o = lax.map(one_block, qb)
    return o.transpose(1, 0, 2, 3, 4, 5).reshape(B, S, D_ATTN)


def centred_depthwise_conv(x, w, b):
    y = lax.conv_general_dilated(
        x, w.astype(x.dtype)[:, None, :], window_strides=(1,), padding=[CONV_PAD],
        dimension_numbers=('NWC', 'WIO', 'NWC'), feature_group_count=x.shape[-1])
    return y + b.astype(x.dtype)


def _linear_combine(left, right):
    a1, b1 = left
    a2, b2 = right
    return a1 * a2, a2 * b1 + b2


def rglru_scan(x, wa, ba, wx, bx, lam, reverse):
    B, S = x.shape[0], x.shape[1]
    xg = x.reshape(B, S, N_LRU_BLOCKS, LRU_BLOCK)
    r = jax.nn.sigmoid((jnp.einsum('bsnc,ncd->bsnd', xg, wa).reshape(B, S, D_LRU) + ba).astype(jnp.float32))
    i = jax.nn.sigmoid((jnp.einsum('bsnc,ncd->bsnd', xg, wx).reshape(B, S, D_LRU) + bx).astype(jnp.float32))
    log_a = -LRU_C * jax.nn.softplus(-lam.astype(jnp.float32)) * r
    a = jnp.exp(log_a)
    u = jnp.sqrt(-jnp.expm1(2.0 * log_a)) * (i * x.astype(jnp.float32))
    _, h = lax.associative_scan(_linear_combine, (a, u), axis=1, reverse=reverse)
    return h


def recurrent_group(x_br, y_br, conv_w, conv_b, lru_wa, lru_ba, lru_wx, lru_bx, lru_lambda):
    xc = centred_depthwise_conv(x_br, conv_w, conv_b)
    h_fwd = rglru_scan(xc, lru_wa[0], lru_ba[0], lru_wx[0], lru_bx[0], lru_lambda[0], False)
    h_bwd = rglru_scan(xc, lru_wa[1], lru_ba[1], lru_wx[1], lru_bx[1], lru_lambda[1], True)
    return ((h_fwd + h_bwd) * jax.nn.gelu(y_br.astype(jnp.float32))).astype(x_br.dtype)


def encoder_layer(x, norm_mix_g, w_in, q_norm_g, k_norm_g, conv_w, conv_b, lru_wa, lru_ba,
                  lru_wx, lru_bx, lru_lambda, w_out, norm_mlp_g, w_up, w_down):
    B, S = x.shape[0], x.shape[1]
    h = rms_norm(x, norm_mix_g)
    proj = h @ w_in
    splits = [D_ATTN, D_ATTN + D_KV, D_ATTN + 2 * D_KV, D_ATTN + 2 * D_KV + D_LRU]
    q, k, v, x_br, y_br = jnp.split(proj, splits, axis=-1)
    q = rms_norm(q.reshape(B, S, N_Q_HEADS, HEAD_DIM), q_norm_g)
    k = rms_norm(k.reshape(B, S, N_KV_HEADS, HEAD_DIM), k_norm_g)
    v = v.reshape(B, S, N_KV_HEADS, HEAD_DIM)
    tables = axial_rope_tables(S)
    attn = bidir_gqa(apply_axial_rope(q, tables), apply_axial_rope(k, tables), v)
    rec = recurrent_group(x_br, y_br, conv_w, conv_b, lru_wa, lru_ba, lru_wx, lru_bx, lru_lambda)
    x = x + jnp.concatenate([attn, rec], axis=-1) @ w_out
    h = rms_norm(x, norm_mlp_g)
    x = x + jnp.square(jax.nn.relu(h @ w_up)) @ w_down
    return x


def encoder_trunk(x, norm_mix_g, w_in, q_norm_g, k_norm_g, conv_w, conv_b, lru_wa, lru_ba,
                  lru_wx, lru_bx, lru_lambda, w_out, norm_mlp_g, w_up, w_down, norm_final_g):
    for l in range(DEPTH):
        x = encoder_layer(x, norm_mix_g[l], w_in[l], q_norm_g[l], k_norm_g[l], conv_w[l], conv_b[l],
                          lru_wa[l], lru_ba[l], lru_wx[l], lru_bx[l], lru_lambda[l], w_out[l],
                          norm_mlp_g[l], w_up[l], w_down[l])
    return rms_norm(x, norm_final_g)


def setup_inputs(seed: int = 0) -> dict:
    key = jax.random.key(seed)
    ks = jax.random.split(key, 20)
    f32 = jnp.float32

    def nrm(k, shape, scale):
        return jax.random.normal(k, shape, f32) * scale

    a0 = jax.random.uniform(ks[12], (DEPTH, 2, D_LRU), f32, minval=0.9, maxval=0.999)
    return {
        "x_prompt": nrm(ks[0], (BATCH, SEQ, D_MODEL), 1.0),
        "x_sample": nrm(ks[1], (DEC_BATCH, DEC_SEQ, D_MODEL), 1.0),
        "norm_mix_g": 1.0 + nrm(ks[2], (DEPTH, D_MODEL), 0.02),
        "w_in": nrm(ks[3], (DEPTH, D_MODEL, D_IN), D_MODEL ** -0.5),
        "q_norm_g": 1.0 + nrm(ks[4], (DEPTH, HEAD_DIM), 0.02),
        "k_norm_g": 1.0 + nrm(ks[5], (DEPTH, HEAD_DIM), 0.02),
        "conv_w": nrm(ks[6], (DEPTH, CONV_W, D_LRU), CONV_W ** -0.5),
        "conv_b": nrm(ks[7], (DEPTH, D_LRU), 0.01),
        "lru_wa": nrm(ks[8], (DEPTH, 2, N_LRU_BLOCKS, LRU_BLOCK, LRU_BLOCK), LRU_BLOCK ** -0.5),
        "lru_ba": nrm(ks[9], (DEPTH, 2, D_LRU), 0.01),
        "lru_wx": nrm(ks[10], (DEPTH, 2, N_LRU_BLOCKS, LRU_BLOCK, LRU_BLOCK), LRU_BLOCK ** -0.5),
        "lru_bx": nrm(ks[11], (DEPTH, 2, D_LRU), 0.01),
        "lru_lambda": jnp.log(a0) - jnp.log1p(-a0),
        "w_out": nrm(ks[13], (DEPTH, D_MIX, D_MODEL), D_MIX ** -0.5),
        "norm_mlp_g": 1.0 + nrm(ks[14], (DEPTH, D_MODEL), 0.02),
        "w_up": nrm(ks[15], (DEPTH, D_MODEL, D_FF), D_MODEL ** -0.5),
        "w_down": nrm(ks[16], (DEPTH, D_FF, D_MODEL), D_FF ** -0.5),
        "norm_final_g": 1.0 + nrm(ks[17], (D_MODEL,), 0.02),
    }


def reference(x_prompt, x_sample, norm_mix_g, w_in, q_norm_g, k_norm_g, conv_w, conv_b, lru_wa,
              lru_ba, lru_wx, lru_bx, lru_lambda, w_out, norm_mlp_g, w_up, w_down, norm_final_g):
    y_prompt = encoder_trunk(x_prompt, norm_mix_g, w_in, q_norm_g, k_norm_g, conv_w, conv_b, lru_wa,
                             lru_ba, lru_wx, lru_bx, lru_lambda, w_out, norm_mlp_g, w_up, w_down,
                             norm_final_g)
    y_sample = encoder_trunk(x_sample, norm_mix_g, w_in, q_norm_g, k_norm_g, conv_w, conv_b, lru_wa,
                             lru_ba, lru_wx, lru_bx, lru_lambda, w_out, norm_mlp_g, w_up, w_down,
                             norm_final_g)
    return (y_prompt, y_sample)
```

```python
import functools
import math

import jax
import jax.numpy as jnp
from jax import lax
from jax.experimental import pallas as pl
from jax.experimental.pallas import tpu as pltpu

F32 = jnp.float32
BF16 = jnp.bfloat16

D_MODEL = 1024
HEAD_DIM = 64
N_Q_HEADS = 8
N_KV_HEADS = 2
Q_PER_KV = N_Q_HEADS // N_KV_HEADS
D_ATTN = N_Q_HEADS * HEAD_DIM
D_KV = N_KV_HEADS * HEAD_DIM
D_LRU = 512
N_LRU_BLOCKS = 8
LRU_BLOCK = D_LRU // N_LRU_BLOCKS
LRU_C = 8.0
CONV_W = 4
D_FF = 4 * D_MODEL
GRID_W = 64
ROPE_HALF = HEAD_DIM // 2
ROPE_QUARTER = ROPE_HALF // 2
ROPE_THETA = 10000.0
EPS = 1e-6

LANES = 128
SUBLANES = 8
MXU_DIM = 256

D_KDUP = 2 * D_KV
D_QK = D_ATTN + D_KDUP
D_PROJ = D_QK + D_KDUP + 2 * D_LRU

TM_PROJ = 512
TQ = 256
TK = 512
T_LRU = 512
TM_MLP = 512
FF_CHUNK = 1024
VMEM_LIMIT = 56 * 1024 * 1024

NEG_BIG = -1e30


def _rms_scale(x):
    return lax.rsqrt(jnp.mean(x * x, axis=-1, keepdims=True) + EPS)


def _proj_kernel(x_ref, g_ref, w_ref, gqk_ref, seg_ref, cos_ref, sin_ref,
                 q_ref, k_ref, v_ref, xbr_ref, ybr_ref):
    x = x_ref[...]
    h = (x * _rms_scale(x) * g_ref[...]).astype(BF16)
    proj = jnp.dot(h, w_ref[...], preferred_element_type=F32)

    cos = cos_ref[...]
    sin = sin_ref[...]
    lane = lax.broadcasted_iota(jnp.int32, cos.shape, 1)
    first_half = (lane & (ROPE_HALF - 1)) < ROPE_QUARTER
    seg = seg_ref[...]

    for c in range(D_QK // MXU_DIM):
        t = proj[:, c * MXU_DIM:(c + 1) * MXU_DIM]
        sq = t * t
        sq_hi = sq.astype(BF16)
        sq_lo = (sq - sq_hi.astype(F32)).astype(BF16)
        ssum = (jnp.dot(sq_hi, seg, preferred_element_type=F32)
                + jnp.dot(sq_lo, seg, preferred_element_type=F32))
        tn = t * lax.rsqrt(ssum * (1.0 / HEAD_DIM) + EPS) * gqk_ref[:, c * MXU_DIM:(c + 1) * MXU_DIM]
        for s in range(MXU_DIM // LANES):
            u = tn[:, s * LANES:(s + 1) * LANES]
            partner = jnp.where(first_half,
                                pltpu.roll(u, LANES - ROPE_QUARTER, 1),
                                pltpu.roll(u, ROPE_QUARTER, 1))
            r = (u * cos + partner * sin).astype(BF16)
            col = c * MXU_DIM + s * LANES
            if col < D_ATTN:
                q_ref[:, col:col + LANES] = r
            else:
                k_ref[:, col - D_ATTN:col - D_ATTN + LANES] = r

    v_ref[...] = proj[:, D_QK:D_QK + D_KDUP].astype(BF16)
    xbr_ref[...] = proj[:, D_QK + D_KDUP:D_QK + D_KDUP + D_LRU].astype(BF16)
    ybr_ref[...] = proj[:, D_QK + D_KDUP + D_LRU:].astype(BF16)


def _proj_call(x2d, g_mix, w_ext, gqk, seg, cos_t, sin_t, seq_len):
    n_tok = x2d.shape[0]
    tiles_per_seq = seq_len // TM_PROJ
    const = lambda i: (0, 0)
    tok = lambda i: (i, 0)
    pos = lambda i: (i % tiles_per_seq, 0)
    return pl.pallas_call(
        _proj_kernel,
        grid=(n_tok // TM_PROJ,),
        in_specs=[
            pl.BlockSpec((TM_PROJ, D_MODEL), tok),
            pl.BlockSpec((1, D_MODEL), const),
            pl.BlockSpec((D_MODEL, D_PROJ), const),
            pl.BlockSpec((1, D_QK), const),
            pl.BlockSpec((MXU_DIM, MXU_DIM), const),
            pl.BlockSpec((TM_PROJ, LANES), pos),
            pl.BlockSpec((TM_PROJ, LANES), pos),
        ],
        out_specs=[
            pl.BlockSpec((TM_PROJ, D_ATTN), tok),
            pl.BlockSpec((TM_PROJ, D_KDUP), tok),
            pl.BlockSpec((TM_PROJ, D_KDUP), tok),
            pl.BlockSpec((TM_PROJ, D_LRU), tok),
            pl.BlockSpec((TM_PROJ, D_LRU), tok),
        ],
        out_shape=[
            jax.ShapeDtypeStruct((n_tok, D_ATTN), BF16),
            jax.ShapeDtypeStruct((n_tok, D_KDUP), BF16),
            jax.ShapeDtypeStruct((n_tok, D_KDUP), BF16),
            jax.ShapeDtypeStruct((n_tok, D_LRU), BF16),
            jax.ShapeDtypeStruct((n_tok, D_LRU), BF16),
        ],
        compiler_params=pltpu.CompilerParams(vmem_limit_bytes=VMEM_LIMIT),
        name="proj",
    )(x2d, g_mix, w_ext, gqk, seg, cos_t, sin_t)


def _attn_kernel(q_ref, k_ref, v_ref, o_ref, qs_ref, m_ref, l_ref, acc_ref, *, n_kv):
    j = pl.program_id(3)
    lane = lax.broadcasted_iota(jnp.int32, (TQ, LANES), 1)
    low_half = lane < HEAD_DIM

    @pl.when(j == 0)
    def _():
        keep_low = low_half.astype(F32).astype(BF16)
        keep_high = (1.0 - low_half.astype(F32)).astype(BF16)
        for hh in range(Q_PER_KV):
            pair = q_ref[:, (hh // 2) * LANES:(hh // 2 + 1) * LANES]
            qs_ref[hh * TQ:(hh + 1) * TQ, :] = pair * (keep_low if hh % 2 == 0 else keep_high)
        m_ref[...] = jnp.full(m_ref.shape, NEG_BIG, F32)
        l_ref[...] = jnp.zeros(l_ref.shape, F32)
        acc_ref[...] = jnp.zeros(acc_ref.shape, F32)

    s = lax.dot_general(qs_ref[...], k_ref[...], (((1,), (1,)), ((), ())),
                        preferred_element_type=F32)
    m_prev = m_ref[...]
    m_new = jnp.maximum(m_prev, jnp.max(s, axis=1, keepdims=True))
    alpha = jnp.exp2(m_prev - m_new)
    p = jnp.exp2(s - jnp.tile(m_new, (1, TK // LANES)))
    l_ref[...] = alpha * l_ref[...] + jnp.sum(p, axis=1, keepdims=True)
    acc_ref[...] = alpha * acc_ref[...] + jnp.dot(p.astype(BF16), v_ref[...],
                                                  preferred_element_type=F32)
    m_ref[...] = m_new

    @pl.when(j == n_kv - 1)
    def _():
        o = acc_ref[...] / l_ref[...]
        for pr in range(Q_PER_KV // 2):
            even = o[(2 * pr) * TQ:(2 * pr + 1) * TQ]
            odd = o[(2 * pr + 1) * TQ:(2 * pr + 2) * TQ]
            o_ref[:, pr * LANES:(pr + 1) * LANES] = jnp.where(low_half, even, odd).astype(BF16)


def _attn_call(q, kd, vd):
    b, s, _ = q.shape
    n_q, n_kv = s // TQ, s // TK
    rows = Q_PER_KV * TQ
    return pl.pallas_call(
        functools.partial(_attn_kernel, n_kv=n_kv),
        grid=(b, N_KV_HEADS, n_q, n_kv),
        in_specs=[
            pl.BlockSpec((None, TQ, Q_PER_KV * HEAD_DIM), lambda bi, g, i, j: (bi, i, g)),
            pl.BlockSpec((None, TK, LANES), lambda bi, g, i, j: (bi, j, g)),
            pl.BlockSpec((None, TK, LANES), lambda bi, g, i, j: (bi, j, g)),
        ],
        out_specs=pl.BlockSpec((None, TQ, Q_PER_KV * HEAD_DIM), lambda bi, g, i, j: (bi, i, g)),
        out_shape=jax.ShapeDtypeStruct((b, s, D_ATTN), BF16),
        scratch_shapes=[
            pltpu.VMEM((rows, LANES), BF16),
            pltpu.VMEM((rows, LANES), F32),
            pltpu.VMEM((rows, LANES), F32),
            pltpu.VMEM((rows, LANES), F32),
        ],
        compiler_params=pltpu.CompilerParams(vmem_limit_bytes=VMEM_LIMIT),
        name="attn",
    )(q, kd, vd)


def _lru_gates(xe_ref, x_ref, prev_ref, next_ref, cw_ref, cb_ref, wa_ref, ba_ref, wx_ref, bx_ref,
               lam_ref, a_ref, u_ref, has_prev, has_next):
    t_rows = x_ref.shape[0]
    xe_ref[0:SUBLANES, :] = prev_ref[...].astype(F32) * has_prev
    xe_ref[SUBLANES:SUBLANES + t_rows, :] = x_ref[...].astype(F32)
    xe_ref[SUBLANES + t_rows:, :] = next_ref[...].astype(F32) * has_next
    xc = cb_ref[...]
    for w in range(CONV_W):
        off = SUBLANES - 2 + w
        xc = xc + cw_ref[w:w + 1, :] * xe_ref[off:off + t_rows, :]
    xcb = xc.astype(BF16)
    half = D_LRU // 2
    r_parts, i_parts = [], []
    for c in range(2):
        xh = xcb[:, c * half:(c + 1) * half]
        r_parts.append(jnp.dot(xh, wa_ref[c], preferred_element_type=F32))
        i_parts.append(jnp.dot(xh, wx_ref[c], preferred_element_type=F32))
    r = jax.nn.sigmoid(jnp.concatenate(r_parts, axis=1) + ba_ref[...])
    i = jax.nn.sigmoid(jnp.concatenate(i_parts, axis=1) + bx_ref[...])
    lam = lam_ref[...]
    neg_softplus = -(jnp.maximum(-lam, 0.0) + jnp.log1p(jnp.exp(-jnp.abs(lam))))
    log_a = (LRU_C * neg_softplus) * r
    a_ref[...] = jnp.exp(log_a)
    th = jnp.tanh(log_a)
    u_ref[...] = jnp.sqrt(-2.0 * th / (1.0 - th)) * (i * xc)


def _scan_groups(a_ref, u_ref, h_ref, carry_ref, n_groups, reverse):
    row = lax.broadcasted_iota(jnp.int32, (SUBLANES, D_LRU), 0)

    def body(g, h_in):
        gi = (n_groups - 1 - g) if reverse else g
        start = pl.multiple_of(gi * SUBLANES, SUBLANES)
        a = a_ref[pl.ds(start, SUBLANES), :]
        u = u_ref[pl.ds(start, SUBLANES), :]
        for k in (1, 2, 4):
            if reverse:
                valid = row < SUBLANES - k
                shift = SUBLANES - k
            else:
                valid = row >= k
                shift = k
            a_s = jnp.where(valid, pltpu.roll(a, shift, 0), 1.0)
            u_s = jnp.where(valid, pltpu.roll(u, shift, 0), 0.0)
            u = u + a * u_s
            a = a * a_s
        h = u + a * h_in
        h_ref[pl.ds(start, SUBLANES), :] = h
        edge = h[0:1, :] if reverse else h[SUBLANES - 1:SUBLANES, :]
        return jnp.broadcast_to(edge, (SUBLANES, D_LRU))

    carry_ref[...] = lax.fori_loop(0, n_groups, body, carry_ref[...])


def _lru_fwd_kernel(x_ref, prev_ref, next_ref, cw_ref, cb_ref, wa_ref, ba_ref, wx_ref, bx_ref, lam_ref,
                    hf_ref, xe_ref, a_ref, u_ref, carry_ref, *, n_tiles):
    i = pl.program_id(1)

    @pl.when(i == 0)
    def _():
        carry_ref[...] = jnp.zeros(carry_ref.shape, F32)

    has_prev = (i > 0).astype(F32)
    has_next = (i < n_tiles - 1).astype(F32)
    _lru_gates(xe_ref, x_ref, prev_ref, next_ref, cw_ref, cb_ref, wa_ref, ba_ref, wx_ref, bx_ref,
               lam_ref, a_ref, u_ref, has_prev, has_next)
    _scan_groups(a_ref, u_ref, hf_ref, carry_ref, T_LRU // SUBLANES, reverse=False)


def _lru_bwd_kernel(x_ref, prev_ref, next_ref, y_ref, hf_ref, cw_ref, cb_ref, wa_ref, ba_ref, wx_ref,
                    bx_ref, lam_ref, o_ref, xe_ref, a_ref, u_ref, hb_ref, carry_ref, *, n_tiles):
    i = n_tiles - 1 - pl.program_id(1)

    @pl.when(i == n_tiles - 1)
    def _():
        carry_ref[...] = jnp.zeros(carry_ref.shape, F32)

    has_prev = (i > 0).astype(F32)
    has_next = (i < n_tiles - 1).astype(F32)
    _lru_gates(xe_ref, x_ref, prev_ref, next_ref, cw_ref, cb_ref, wa_ref, ba_ref, wx_ref, bx_ref,
               lam_ref, a_ref, u_ref, has_prev, has_next)
    _scan_groups(a_ref, u_ref, hb_ref, carry_ref, T_LRU // SUBLANES, reverse=True)
    y = y_ref[...].astype(F32)
    gelu = 0.5 * y * (1.0 + jnp.tanh(math.sqrt(2.0 / math.pi) * (y + 0.044715 * (y * y * y))))
    o_ref[...] = ((hf_ref[...] + hb_ref[...]) * gelu).astype(BF16)


def _lru_specs(n_tiles, reverse):
    blocks_per_tile = T_LRU // SUBLANES
    n_blocks = n_tiles * blocks_per_tile
    tile_of = (lambda i: n_tiles - 1 - i) if reverse else (lambda i: i)
    tile = lambda b, i: (b, tile_of(i), 0)
    prev = lambda b, i: (b, jnp.maximum(tile_of(i) * blocks_per_tile - 1, 0), 0)
    nxt = lambda b, i: (b, jnp.minimum((tile_of(i) + 1) * blocks_per_tile, n_blocks - 1), 0)
    c2 = lambda b, i: (0, 0)
    c3 = lambda b, i: (0, 0, 0)
    half = D_LRU // 2
    x_specs = [
        pl.BlockSpec((None, T_LRU, D_LRU), tile),
        pl.BlockSpec((None, SUBLANES, D_LRU), prev),
        pl.BlockSpec((None, SUBLANES, D_LRU), nxt),
    ]
    w_specs = [
        pl.BlockSpec((CONV_W, D_LRU), c2),
        pl.BlockSpec((1, D_LRU), c2),
        pl.BlockSpec((2, half, half), c3),
        pl.BlockSpec((1, D_LRU), c2),
        pl.BlockSpec((2, half, half), c3),
        pl.BlockSpec((1, D_LRU), c2),
        pl.BlockSpec((1, D_LRU), c2),
    ]
    return x_specs, w_specs, tile


def _lru_call(xbr, ybr, conv_w, conv_b, wa, ba, wx, bx, lam):
    b, s, _ = xbr.shape
    n_tiles = s // T_LRU
    scratch = [
        pltpu.VMEM((T_LRU + 2 * SUBLANES, D_LRU), F32),
        pltpu.VMEM((T_LRU, D_LRU), F32),
        pltpu.VMEM((T_LRU, D_LRU), F32),
    ]
    carry = pltpu.VMEM((SUBLANES, D_LRU), F32)
    params = pltpu.CompilerParams(vmem_limit_bytes=VMEM_LIMIT)

    x_specs, w_specs, tile = _lru_specs(n_tiles, reverse=False)
    h_fwd = pl.pallas_call(
        functools.partial(_lru_fwd_kernel, n_tiles=n_tiles),
        grid=(b, n_tiles),
        in_specs=x_specs + w_specs,
        out_specs=pl.BlockSpec((None, T_LRU, D_LRU), tile),
        out_shape=jax.ShapeDtypeStruct((b, s, D_LRU), F32),
        scratch_shapes=scratch + [carry],
        compiler_params=params,
        name="lru_fwd",
    )(xbr, xbr, xbr, conv_w, conv_b, wa[0], ba[0:1], wx[0], bx[0:1], lam[0:1])

    x_specs, w_specs, tile = _lru_specs(n_tiles, reverse=True)
    return pl.pallas_call(
        functools.partial(_lru_bwd_kernel, n_tiles=n_tiles),
        grid=(b, n_tiles),
        in_specs=x_specs + [pl.BlockSpec((None, T_LRU, D_LRU), tile),
                            pl.BlockSpec((None, T_LRU, D_LRU), tile)] + w_specs,
        out_specs=pl.BlockSpec((None, T_LRU, D_LRU), tile),
        out_shape=jax.ShapeDtypeStruct((b, s, D_LRU), BF16),
        scratch_shapes=scratch + [pltpu.VMEM((T_LRU, D_LRU), F32), carry],
        compiler_params=params,
        name="lru_bwd",
    )(xbr, xbr, xbr, ybr, h_fwd, conv_w, conv_b, wa[1], ba[1:2], wx[1], bx[1:2], lam[1:2])


def _mlp_kernel(x_ref, attn_ref, rec_ref, wo_ref, g2_ref, wup_ref, wdn_ref, gf_ref, o_ref):
    x1 = (x_ref[...]
          + jnp.dot(attn_ref[...], wo_ref[0:D_ATTN, :], preferred_element_type=F32)
          + jnp.dot(rec_ref[...], wo_ref[D_ATTN:, :], preferred_element_type=F32))
    h = (x1 * _rms_scale(x1) * g2_ref[...]).astype(BF16)
    mlp = None
    for c in range(D_FF // FF_CHUNK):
        up = jnp.dot(h, wup_ref[:, c * FF_CHUNK:(c + 1) * FF_CHUNK], preferred_element_type=F32)
        act = jnp.square(jnp.maximum(up, 0.0)).astype(BF16)
        down = jnp.dot(act, wdn_ref[c * FF_CHUNK:(c + 1) * FF_CHUNK, :], preferred_element_type=F32)
        mlp = down if mlp is None else mlp + down
    x2 = x1 + mlp
    o_ref[...] = x2 * _rms_scale(x2) * gf_ref[...]


def _mlp_call(x2d, attn2d, rec2d, w_out, g_mlp, w_up, w_down, g_final):
    n_tok = x2d.shape[0]
    const = lambda i: (0, 0)
    tok = lambda i: (i, 0)
    resident = pl.Buffered(1)
    return pl.pallas_call(
        _mlp_kernel,
        grid=(n_tok // TM_MLP,),
        in_specs=[
            pl.BlockSpec((TM_MLP, D_MODEL), tok),
            pl.BlockSpec((TM_MLP, D_ATTN), tok),
            pl.BlockSpec((TM_MLP, D_LRU), tok),
            pl.BlockSpec((D_MODEL, D_MODEL), const, pipeline_mode=resident),
            pl.BlockSpec((1, D_MODEL), const),
            pl.BlockSpec((D_MODEL, D_FF), const, pipeline_mode=resident),
            pl.BlockSpec((D_FF, D_MODEL), const, pipeline_mode=resident),
            pl.BlockSpec((1, D_MODEL), const),
        ],
        out_specs=pl.BlockSpec((TM_MLP, D_MODEL), tok),
        out_shape=jax.ShapeDtypeStruct((n_tok, D_MODEL), F32),
        compiler_params=pltpu.CompilerParams(vmem_limit_bytes=VMEM_LIMIT),
        name="mlp",
    )(x2d, attn2d, rec2d, w_out, g_mlp, w_up, w_down, g_final)


def _rope_tables(seq_len):
    rows = seq_len // GRID_W
    row_ids = jnp.repeat(jnp.arange(rows), GRID_W).astype(F32)
    col_ids = jnp.tile(jnp.arange(GRID_W), rows).astype(F32)
    inv_freq = ROPE_THETA ** (-jnp.arange(0, ROPE_HALF, 2, dtype=F32) / ROPE_HALF)
    ang_r = row_ids[:, None] * inv_freq
    ang_c = col_ids[:, None] * inv_freq
    cos = jnp.concatenate([jnp.cos(ang_r)] * 2 + [jnp.cos(ang_c)] * 2, axis=-1)
    sin = jnp.concatenate([-jnp.sin(ang_r), jnp.sin(ang_r), -jnp.sin(ang_c), jnp.sin(ang_c)], axis=-1)
    return jnp.tile(cos, (1, LANES // HEAD_DIM)), jnp.tile(sin, (1, LANES // HEAD_DIM))


def _block_diag_halves(w):
    per_half = N_LRU_BLOCKS // 2
    halves = [jax.scipy.linalg.block_diag(*[w[c * per_half + n] for n in range(per_half)])
              for c in range(2)]
    return jnp.stack(halves).astype(BF16)


def _layer(x, params, seq_tables):
    b, s, _ = x.shape
    x2d = x.reshape(b * s, D_MODEL)
    q, kd, vd, xbr, ybr = _proj_call(x2d, params["g_mix"], params["w_ext"], params["gqk"],
                                     params["seg"], seq_tables[0], seq_tables[1], s)
    attn = _attn_call(q.reshape(b, s, D_ATTN), kd.reshape(b, s, D_KDUP), vd.reshape(b, s, D_KDUP))
    rec = _lru_call(xbr.reshape(b, s, D_LRU), ybr.reshape(b, s, D_LRU), params["conv_w"],
                    params["conv_b"], params["wa"], params["ba"], params["wx"], params["bx"],
                    params["lam"])
    y = _mlp_call(x2d, attn.reshape(b * s, D_ATTN), rec.reshape(b * s, D_LRU), params["w_out"],
                  params["g_mlp"], params["w_up"], params["w_down"], params["g_final"])
    return y.reshape(b, s, D_MODEL)


def kernel(x_prompt, x_sample, norm_mix_g, w_in, q_norm_g, k_norm_g, conv_w, conv_b, lru_wa, lru_ba,
           lru_wx, lru_bx, lru_lambda, w_out, norm_mlp_g, w_up, w_down, norm_final_g):
    l = 0
    wi = w_in[l]
    wq = wi[:, :D_ATTN]
    wk = [wi[:, D_ATTN + g * HEAD_DIM:D_ATTN + (g + 1) * HEAD_DIM] for g in range(N_KV_HEADS)]
    wv = [wi[:, D_ATTN + D_KV + g * HEAD_DIM:D_ATTN + D_KV + (g + 1) * HEAD_DIM]
          for g in range(N_KV_HEADS)]
    w_ext = jnp.concatenate([wq, wk[0], wk[0], wk[1], wk[1], wv[0], wv[0], wv[1], wv[1],
                             wi[:, D_ATTN + 2 * D_KV:]], axis=1).astype(BF16)
    q_gain = q_norm_g[l] * (HEAD_DIM ** -0.5 * math.log2(math.e))
    gqk = jnp.concatenate([jnp.tile(q_gain, N_Q_HEADS), jnp.tile(k_norm_g[l], 2 * N_KV_HEADS)])
    seg = jax.scipy.linalg.block_diag(*[jnp.ones((HEAD_DIM, HEAD_DIM), F32)] * (MXU_DIM // HEAD_DIM))
    params = {
        "g_mix": norm_mix_g[l][None, :],
        "w_ext": w_ext,
        "gqk": gqk[None, :],
        "seg": seg.astype(BF16),
        "conv_w": conv_w[l],
        "conv_b": conv_b[l][None, :],
        "wa": jnp.stack([_block_diag_halves(lru_wa[l, d]) for d in range(2)]),
        "ba": lru_ba[l],
        "wx": jnp.stack([_block_diag_halves(lru_wx[l, d]) for d in range(2)]),
        "bx": lru_bx[l],
        "lam": lru_lambda[l],
        "w_out": w_out[l].astype(BF16),
        "g_mlp": norm_mlp_g[l][None, :],
        "w_up": w_up[l].astype(BF16),
        "w_down": w_down[l].astype(BF16),
        "g_final": norm_final_g[None, :],
    }
    outs = []
    for x in (x_prompt, x_sample):
        outs.append(_layer(x, params, _rope_tables(x.shape[1])))
    return tuple(outs)
```

```python
import functools
import math

import jax
import jax.numpy as jnp
from jax import lax
from jax.experimental import pallas as pl
from jax.experimental.pallas import tpu as pltpu

F32 = jnp.float32
BF16 = jnp.bfloat16

D_MODEL = 1024
HEAD_DIM = 64
N_Q_HEADS = 8
N_KV_HEADS = 2
Q_PER_KV = N_Q_HEADS // N_KV_HEADS
D_ATTN = N_Q_HEADS * HEAD_DIM
D_KV = N_KV_HEADS * HEAD_DIM
D_LRU = 512
N_LRU_BLOCKS = 8
LRU_BLOCK = D_LRU // N_LRU_BLOCKS
LRU_C = 8.0
CONV_W = 4
D_FF = 4 * D_MODEL
GRID_W = 64
ROPE_HALF = HEAD_DIM // 2
ROPE_QUARTER = ROPE_HALF // 2
ROPE_THETA = 10000.0
EPS = 1e-6

LANES = 128
SUBLANES = 8
MXU_DIM = 256

D_KDUP = 2 * D_KV
D_QK = D_ATTN + D_KDUP
D_PROJ = D_QK + D_KDUP + 2 * D_LRU

TM_PROJ = 512
TQ = 512
TK_STEP = 2048
TK_CHUNK = 512
T_LRU = 512
LRU_CHUNK = T_LRU // SUBLANES + 4
LRU_PAD_ROWS = SUBLANES * LRU_CHUNK
LRU_SLABS = D_LRU // LANES
LRU_UNROLL = 4
TM_MLP = 512
FF_CHUNK = 1024
VMEM_LIMIT = 56 * 1024 * 1024

NEG_BIG = -1e30


def _rms_scale(x):
    return lax.rsqrt(jnp.mean(x * x, axis=-1, keepdims=True) + EPS)


def _proj_kernel(x_ref, g_ref, w_ref, gqk_ref, seg_ref, cos_ref, sin_ref,
                 q_ref, k_ref, v_ref, xbr_ref, ybr_ref):
    x = x_ref[...]
    h = (x * _rms_scale(x) * g_ref[...]).astype(BF16)
    proj = jnp.dot(h, w_ref[...], preferred_element_type=F32)

    cos = cos_ref[...]
    sin = sin_ref[...]
    lane = lax.broadcasted_iota(jnp.int32, cos.shape, 1)
    first_half = (lane & (ROPE_HALF - 1)) < ROPE_QUARTER
    seg = seg_ref[...]

    for c in range(D_QK // MXU_DIM):
        t = proj[:, c * MXU_DIM:(c + 1) * MXU_DIM]
        sq = t * t
        sq_hi = sq.astype(BF16)
        sq_lo = (sq - sq_hi.astype(F32)).astype(BF16)
        ssum = (jnp.dot(sq_hi, seg, preferred_element_type=F32)
                + jnp.dot(sq_lo, seg, preferred_element_type=F32))
        tn = t * lax.rsqrt(ssum * (1.0 / HEAD_DIM) + EPS) * gqk_ref[:, c * MXU_DIM:(c + 1) * MXU_DIM]
        for s in range(MXU_DIM // LANES):
            u = tn[:, s * LANES:(s + 1) * LANES]
            partner = jnp.where(first_half,
                                pltpu.roll(u, LANES - ROPE_QUARTER, 1),
                                pltpu.roll(u, ROPE_QUARTER, 1))
            r = (u * cos + partner * sin).astype(BF16)
            col = c * MXU_DIM + s * LANES
            if col < D_ATTN:
                q_ref[:, col:col + LANES] = r
            else:
                k_ref[:, col - D_ATTN:col - D_ATTN + LANES] = r

    vlane = lax.broadcasted_iota(jnp.int32, (x.shape[0], D_KDUP), 1)
    v_ref[...] = jnp.where((vlane & (LANES - 1)) < HEAD_DIM, proj[:, D_QK:D_QK + D_KDUP], 1.0).astype(BF16)
    xbr_ref[...] = proj[:, D_QK + D_KDUP:D_QK + D_KDUP + D_LRU].astype(BF16)
    ybr_ref[...] = proj[:, D_QK + D_KDUP + D_LRU:].astype(BF16)


def _proj_call(x2d, g_mix, w_ext, gqk, seg, cos_t, sin_t, seq_len):
    n_tok = x2d.shape[0]
    tiles_per_seq = seq_len // TM_PROJ
    const = lambda i: (0, 0)
    tok = lambda i: (i, 0)
    pos = lambda i: (i % tiles_per_seq, 0)
    return pl.pallas_call(
        _proj_kernel,
        grid=(n_tok // TM_PROJ,),
        in_specs=[
            pl.BlockSpec((TM_PROJ, D_MODEL), tok),
            pl.BlockSpec((1, D_MODEL), const),
            pl.BlockSpec((D_MODEL, D_PROJ), const),
            pl.BlockSpec((1, D_QK), const),
            pl.BlockSpec((MXU_DIM, MXU_DIM), const),
            pl.BlockSpec((TM_PROJ, LANES), pos),
            pl.BlockSpec((TM_PROJ, LANES), pos),
        ],
        out_specs=[
            pl.BlockSpec((TM_PROJ, D_ATTN), tok),
            pl.BlockSpec((TM_PROJ, D_KDUP), tok),
            pl.BlockSpec((TM_PROJ, D_KDUP), tok),
            pl.BlockSpec((TM_PROJ, D_LRU), tok),
            pl.BlockSpec((TM_PROJ, D_LRU), tok),
        ],
        out_shape=[
            jax.ShapeDtypeStruct((n_tok, D_ATTN), BF16),
            jax.ShapeDtypeStruct((n_tok, D_KDUP), BF16),
            jax.ShapeDtypeStruct((n_tok, D_KDUP), BF16),
            jax.ShapeDtypeStruct((n_tok, D_LRU), BF16),
            jax.ShapeDtypeStruct((n_tok, D_LRU), BF16),
        ],
        compiler_params=pltpu.CompilerParams(vmem_limit_bytes=VMEM_LIMIT),
        name="proj",
    )(x2d, g_mix, w_ext, gqk, seg, cos_t, sin_t)


def _attn_kernel(q_ref, k_ref, v_ref, o_ref, qs_ref, m_ref, acc_ref, *, n_kv):
    j = pl.program_id(3)
    lane = lax.broadcasted_iota(jnp.int32, (TQ, LANES), 1)
    low_half = lane < HEAD_DIM

    @pl.when(j == 0)
    def _():
        keep_low = low_half.astype(F32).astype(BF16)
        keep_high = (1.0 - low_half.astype(F32)).astype(BF16)
        for hh in range(Q_PER_KV):
            pair = q_ref[:, (hh // 2) * LANES:(hh // 2 + 1) * LANES]
            qs_ref[hh * TQ:(hh + 1) * TQ, :] = pair * (keep_low if hh % 2 == 0 else keep_high)
        m_ref[...] = jnp.full(m_ref.shape, NEG_BIG, F32)
        acc_ref[...] = jnp.zeros(acc_ref.shape, F32)

    qs = qs_ref[...]
    m = m_ref[...]
    acc = acc_ref[...]
    for c in range(TK_STEP // TK_CHUNK):
        rows = slice(c * TK_CHUNK, (c + 1) * TK_CHUNK)
        s = lax.dot_general(qs, k_ref[rows, :], (((1,), (1,)), ((), ())),
                            preferred_element_type=F32)
        m_new = jnp.maximum(m, jnp.max(s, axis=1, keepdims=True))
        alpha = jnp.exp2(m - m_new)
        p = jnp.exp2(s - jnp.tile(m_new, (1, TK_CHUNK // LANES))).astype(BF16)
        acc = alpha * acc + jnp.dot(p, v_ref[rows, :], preferred_element_type=F32)
        m = m_new
    m_ref[...] = m
    acc_ref[...] = acc

    @pl.when(j == n_kv - 1)
    def _():
        a = acc_ref[...]
        swapped = pltpu.roll(a, HEAD_DIM, 1)
        for pr in range(Q_PER_KV // 2):
            ev = slice((2 * pr) * TQ, (2 * pr + 1) * TQ)
            od = slice((2 * pr + 1) * TQ, (2 * pr + 2) * TQ)
            num = jnp.where(low_half, a[ev], swapped[od])
            den = jnp.where(low_half, swapped[ev], a[od])
            o_ref[:, pr * LANES:(pr + 1) * LANES] = (num / den).astype(BF16)


def _attn_call(q, kd, va):
    b, s, _ = q.shape
    n_q, n_kv = s // TQ, s // TK_STEP
    rows = Q_PER_KV * TQ
    return pl.pallas_call(
        functools.partial(_attn_kernel, n_kv=n_kv),
        grid=(b, N_KV_HEADS, n_q, n_kv),
        in_specs=[
            pl.BlockSpec((None, TQ, Q_PER_KV * HEAD_DIM), lambda bi, g, i, j: (bi, i, g)),
            pl.BlockSpec((None, TK_STEP, LANES), lambda bi, g, i, j: (bi, j, g)),
            pl.BlockSpec((None, TK_STEP, LANES), lambda bi, g, i, j: (bi, j, g)),
        ],
        out_specs=pl.BlockSpec((None, TQ, Q_PER_KV * HEAD_DIM), lambda bi, g, i, j: (bi, i, g)),
        out_shape=jax.ShapeDtypeStruct((b, s, D_ATTN), BF16),
        scratch_shapes=[
            pltpu.VMEM((rows, LANES), BF16),
            pltpu.VMEM((rows, LANES), F32),
            pltpu.VMEM((rows, LANES), F32),
        ],
        compiler_params=pltpu.CompilerParams(vmem_limit_bytes=VMEM_LIMIT),
        name="attn",
    )(q, kd, va)


def _lru_gates(xe_ref, x_ref, prev_ref, next_ref, cw_ref, cb_ref, wa_ref, ba_ref, wx_ref, bx_ref,
               lam_ref, a_ref, u_ref, has_prev, has_next):
    t_rows = x_ref.shape[0]
    xe_ref[0:SUBLANES, :] = prev_ref[...].astype(F32) * has_prev
    xe_ref[SUBLANES:SUBLANES + t_rows, :] = x_ref[...].astype(F32)
    xe_ref[SUBLANES + t_rows:, :] = next_ref[...].astype(F32) * has_next
    xc = cb_ref[...]
    for w in range(CONV_W):
        off = SUBLANES - 2 + w
        xc = xc + cw_ref[w:w + 1, :] * xe_ref[off:off + t_rows, :]
    xcb = xc.astype(BF16)
    half = D_LRU // 2
    r_parts, i_parts = [], []
    for c in range(2):
        xh = xcb[:, c * half:(c + 1) * half]
        r_parts.append(jnp.dot(xh, wa_ref[c], preferred_element_type=F32))
        i_parts.append(jnp.dot(xh, wx_ref[c], preferred_element_type=F32))
    t_r = jnp.tanh(jnp.concatenate(r_parts, axis=1) + ba_ref[...])
    t_i = jnp.tanh(jnp.concatenate(i_parts, axis=1) + bx_ref[...])
    lam = lam_ref[...]
    neg_softplus = -(jnp.maximum(-lam, 0.0) + jnp.log1p(jnp.exp(-jnp.abs(lam))))
    log_a = (0.5 * LRU_C * neg_softplus) * (1.0 + t_r)
    a = jnp.exp(log_a)
    th = jnp.tanh(log_a)
    u = jnp.sqrt(-0.5 * th / (1.0 - th)) * ((1.0 + t_i) * xc)
    for lt in range(LRU_SLABS):
        a_ref[lt, 0:t_rows, :] = a[:, lt * LANES:(lt + 1) * LANES]
        u_ref[lt, 0:t_rows, :] = u[:, lt * LANES:(lt + 1) * LANES]
        a_ref[lt, t_rows:, :] = jnp.ones((LRU_PAD_ROWS - t_rows, LANES), F32)
        u_ref[lt, t_rows:, :] = jnp.zeros((LRU_PAD_ROWS - t_rows, LANES), F32)


def _sublane_shift(x, k, fill, row, reverse):
    if reverse:
        return jnp.where(row < SUBLANES - k, pltpu.roll(x, SUBLANES - k, 0), fill)
    return jnp.where(row >= k, pltpu.roll(x, k, 0), fill)


def _scan_tile(a_ref, u_ref, h_ref, carry_ref, reverse):
    row = lax.broadcasted_iota(jnp.int32, (SUBLANES, LANES), 0)

    def step_rows(t):
        tt = (LRU_CHUNK - 1 - t) if reverse else t
        return pl.ds(tt, SUBLANES, stride=LRU_CHUNK)

    def reduce_step(t, au):
        rows = step_rows(t)
        out_a, out_u = [], []
        for lt in range(LRU_SLABS):
            a_t = a_ref[lt, rows, :]
            out_u.append(a_t * au[1][lt] + u_ref[lt, rows, :])
            out_a.append(a_t * au[0][lt])
        return tuple(out_a), tuple(out_u)

    ones = tuple(jnp.ones((SUBLANES, LANES), F32) for _ in range(LRU_SLABS))
    zeros = tuple(jnp.zeros((SUBLANES, LANES), F32) for _ in range(LRU_SLABS))
    tot_a, tot_u = lax.fori_loop(0, LRU_CHUNK, reduce_step, (ones, zeros), unroll=LRU_UNROLL)

    h_in = []
    for lt in range(LRU_SLABS):
        a, u = tot_a[lt], tot_u[lt]
        for k in (1, 2, 4):
            u = u + a * _sublane_shift(u, k, 0.0, row, reverse)
            a = a * _sublane_shift(a, k, 1.0, row, reverse)
        carry = carry_ref[lt]
        h_out = u + a * carry
        h_in.append(_sublane_shift(h_out, 1, carry, row, reverse))
        last = h_out[0:1, :] if reverse else h_out[SUBLANES - 1:SUBLANES, :]
        carry_ref[lt] = jnp.broadcast_to(last, (SUBLANES, LANES))

    def scan_step(t, hs):
        rows = step_rows(t)
        out = []
        for lt in range(LRU_SLABS):
            h = a_ref[lt, rows, :] * hs[lt] + u_ref[lt, rows, :]
            h_ref[lt, rows, :] = h
            out.append(h)
        return tuple(out)

    lax.fori_loop(0, LRU_CHUNK, scan_step, tuple(h_in), unroll=LRU_UNROLL)


def _lru_fwd_kernel(x_ref, prev_ref, next_ref, cw_ref, cb_ref, wa_ref, ba_ref, wx_ref, bx_ref, lam_ref,
                    hf_ref, xe_ref, a_ref, u_ref, h_ref, carry_ref, *, n_tiles):
    i = pl.program_id(1)

    @pl.when(i == 0)
    def _():
        carry_ref[...] = jnp.zeros(carry_ref.shape, F32)

    has_prev = (i > 0).astype(F32)
    has_next = (i < n_tiles - 1).astype(F32)
    _lru_gates(xe_ref, x_ref, prev_ref, next_ref, cw_ref, cb_ref, wa_ref, ba_ref, wx_ref, bx_ref,
               lam_ref, a_ref, u_ref, has_prev, has_next)
    _scan_tile(a_ref, u_ref, h_ref, carry_ref, reverse=False)
    for lt in range(LRU_SLABS):
        hf_ref[:, lt * LANES:(lt + 1) * LANES] = h_ref[lt, 0:T_LRU, :]


def _lru_bwd_kernel(x_ref, prev_ref, next_ref, y_ref, hf_ref, cw_ref, cb_ref, wa_ref, ba_ref, wx_ref,
                    bx_ref, lam_ref, o_ref, xe_ref, a_ref, u_ref, h_ref, carry_ref, *, n_tiles):
    i = n_tiles - 1 - pl.program_id(1)

    @pl.when(i == n_tiles - 1)
    def _():
        carry_ref[...] = jnp.zeros(carry_ref.shape, F32)

    has_prev = (i > 0).astype(F32)
    has_next = (i < n_tiles - 1).astype(F32)
    _lru_gates(xe_ref, x_ref, prev_ref, next_ref, cw_ref, cb_ref, wa_ref, ba_ref, wx_ref, bx_ref,
               lam_ref, a_ref, u_ref, has_prev, has_next)
    _scan_tile(a_ref, u_ref, h_ref, carry_ref, reverse=True)
    for lt in range(LRU_SLABS):
        cols = slice(lt * LANES, (lt + 1) * LANES)
        y = y_ref[:, cols].astype(F32)
        gelu = 0.5 * y * (1.0 + jnp.tanh(math.sqrt(2.0 / math.pi) * (y + 0.044715 * (y * y * y))))
        o_ref[:, cols] = ((hf_ref[:, cols] + h_ref[lt, 0:T_LRU, :]) * gelu).astype(BF16)


def _lru_specs(n_tiles, reverse):
    blocks_per_tile = T_LRU // SUBLANES
    n_blocks = n_tiles * blocks_per_tile
    tile_of = (lambda i: n_tiles - 1 - i) if reverse else (lambda i: i)
    tile = lambda b, i: (b, tile_of(i), 0)
    prev = lambda b, i: (b, jnp.maximum(tile_of(i) * blocks_per_tile - 1, 0), 0)
    nxt = lambda b, i: (b, jnp.minimum((tile_of(i) + 1) * blocks_per_tile, n_blocks - 1), 0)
    c2 = lambda b, i: (0, 0)
    c3 = lambda b, i: (0, 0, 0)
    half = D_LRU // 2
    x_specs = [
        pl.BlockSpec((None, T_LRU, D_LRU), tile),
        pl.BlockSpec((None, SUBLANES, D_LRU), prev),
        pl.BlockSpec((None, SUBLANES, D_LRU), nxt),
    ]
    w_specs = [
        pl.BlockSpec((CONV_W, D_LRU), c2),
        pl.BlockSpec((1, D_LRU), c2),
        pl.BlockSpec((2, half, half), c3),
        pl.BlockSpec((1, D_LRU), c2),
        pl.BlockSpec((2, half, half), c3),
        pl.BlockSpec((1, D_LRU), c2),
        pl.BlockSpec((1, D_LRU), c2),
    ]
    return x_specs, w_specs, tile


def _lru_call(xbr, ybr, conv_w, conv_b, wa, ba, wx, bx, lam):
    b, s, _ = xbr.shape
    n_tiles = s // T_LRU
    slab = pltpu.VMEM((LRU_SLABS, LRU_PAD_ROWS, LANES), F32)
    scratch = [pltpu.VMEM((T_LRU + 2 * SUBLANES, D_LRU), F32), slab, slab, slab]
    carry = pltpu.VMEM((LRU_SLABS, SUBLANES, LANES), F32)
    params = pltpu.CompilerParams(vmem_limit_bytes=VMEM_LIMIT)

    x_specs, w_specs, tile = _lru_specs(n_tiles, reverse=False)
    h_fwd = pl.pallas_call(
        functools.partial(_lru_fwd_kernel, n_tiles=n_tiles),
        grid=(b, n_tiles),
        in_specs=x_specs + w_specs,
        out_specs=pl.BlockSpec((None, T_LRU, D_LRU), tile),
        out_shape=jax.ShapeDtypeStruct((b, s, D_LRU), F32),
        scratch_shapes=scratch + [carry],
        compiler_params=params,
        name="lru_fwd",
    )(xbr, xbr, xbr, conv_w, conv_b, wa[0], ba[0:1], wx[0], bx[0:1], lam[0:1])

    x_specs, w_specs, tile = _lru_specs(n_tiles, reverse=True)
    return pl.pallas_call(
        functools.partial(_lru_bwd_kernel, n_tiles=n_tiles),
        grid=(b, n_tiles),
        in_specs=x_specs + [pl.BlockSpec((None, T_LRU, D_LRU), tile),
                            pl.BlockSpec((None, T_LRU, D_LRU), tile)] + w_specs,
        out_specs=pl.BlockSpec((None, T_LRU, D_LRU), tile),
        out_shape=jax.ShapeDtypeStruct((b, s, D_LRU), BF16),
        scratch_shapes=scratch + [carry],
        compiler_params=params,
        name="lru_bwd",
    )(xbr, xbr, xbr, ybr, h_fwd, conv_w, conv_b, wa[1], ba[1:2], wx[1], bx[1:2], lam[1:2])


def _mlp_kernel(x_ref, attn_ref, rec_ref, wo_ref, g2_ref, wup_ref, wdn_ref, gf_ref, o_ref):
    x1 = (x_ref[...]
          + jnp.dot(attn_ref[...], wo_ref[0:D_ATTN, :], preferred_element_type=F32)
          + jnp.dot(rec_ref[...], wo_ref[D_ATTN:, :], preferred_element_type=F32))
    h = (x1 * _rms_scale(x1) * g2_ref[...]).astype(BF16)
    mlp = None
    for c in range(D_FF // FF_CHUNK):
        up = jnp.dot(h, wup_ref[:, c * FF_CHUNK:(c + 1) * FF_CHUNK], preferred_element_type=F32)
        act = jnp.square(jnp.maximum(up, 0.0)).astype(BF16)
        down = jnp.dot(act, wdn_ref[c * FF_CHUNK:(c + 1) * FF_CHUNK, :], preferred_element_type=F32)
        mlp = down if mlp is None else mlp + down
    x2 = x1 + mlp
    o_ref[...] = x2 * _rms_scale(x2) * gf_ref[...]


def _mlp_call(x2d, attn2d, rec2d, w_out, g_mlp, w_up, w_down, g_final):
    n_tok = x2d.shape[0]
    const = lambda i: (0, 0)
    tok = lambda i: (i, 0)
    resident = pl.Buffered(1)
    return pl.pallas_call(
        _mlp_kernel,
        grid=(n_tok // TM_MLP,),
        in_specs=[
            pl.BlockSpec((TM_MLP, D_MODEL), tok),
            pl.BlockSpec((TM_MLP, D_ATTN), tok),
            pl.BlockSpec((TM_MLP, D_LRU), tok),
            pl.BlockSpec((D_MODEL, D_MODEL), const, pipeline_mode=resident),
            pl.BlockSpec((1, D_MODEL), const),
            pl.BlockSpec((D_MODEL, D_FF), const, pipeline_mode=resident),
            pl.BlockSpec((D_FF, D_MODEL), const, pipeline_mode=resident),
            pl.BlockSpec((1, D_MODEL), const),
        ],
        out_specs=pl.BlockSpec((TM_MLP, D_MODEL), tok),
        out_shape=jax.ShapeDtypeStruct((n_tok, D_MODEL), F32),
        compiler_params=pltpu.CompilerParams(vmem_limit_bytes=VMEM_LIMIT),
        name="mlp",
    )(x2d, attn2d, rec2d, w_out, g_mlp, w_up, w_down, g_final)


def _rope_tables(seq_len):
    rows = seq_len // GRID_W
    row_ids = jnp.repeat(jnp.arange(rows), GRID_W).astype(F32)
    col_ids = jnp.tile(jnp.arange(GRID_W), rows).astype(F32)
    inv_freq = ROPE_THETA ** (-jnp.arange(0, ROPE_HALF, 2, dtype=F32) / ROPE_HALF)
    ang_r = row_ids[:, None] * inv_freq
    ang_c = col_ids[:, None] * inv_freq
    cos = jnp.concatenate([jnp.cos(ang_r)] * 2 + [jnp.cos(ang_c)] * 2, axis=-1)
    sin = jnp.concatenate([-jnp.sin(ang_r), jnp.sin(ang_r), -jnp.sin(ang_c), jnp.sin(ang_c)], axis=-1)
    return jnp.tile(cos, (1, LANES // HEAD_DIM)), jnp.tile(sin, (1, LANES // HEAD_DIM))


def _block_diag_halves(w):
    per_half = N_LRU_BLOCKS // 2
    halves = [jax.scipy.linalg.block_diag(*[w[c * per_half + n] for n in range(per_half)])
              for c in range(2)]
    return jnp.stack(halves).astype(BF16)


def _layer(x, params, seq_tables):
    b, s, _ = x.shape
    x2d = x.reshape(b * s, D_MODEL)
    q, kd, va, xbr, ybr = _proj_call(x2d, params["g_mix"], params["w_ext"], params["gqk"],
                                     params["seg"], seq_tables[0], seq_tables[1], s)
    attn = _attn_call(q.reshape(b, s, D_ATTN), kd.reshape(b, s, D_KDUP), va.reshape(b, s, D_KDUP))
    rec = _lru_call(xbr.reshape(b, s, D_LRU), ybr.reshape(b, s, D_LRU), params["conv_w"],
                    params["conv_b"], params["wa"], params["ba"], params["wx"], params["bx"],
                    params["lam"])
    y = _mlp_call(x2d, attn.reshape(b * s, D_ATTN), rec.reshape(b * s, D_LRU), params["w_out"],
                  params["g_mlp"], params["w_up"], params["w_down"], params["g_final"])
    return y.reshape(b, s, D_MODEL)


def kernel(x_prompt, x_sample, norm_mix_g, w_in, q_norm_g, k_norm_g, conv_w, conv_b, lru_wa, lru_ba,
           lru_wx, lru_bx, lru_lambda, w_out, norm_mlp_g, w_up, w_down, norm_final_g):
    l = 0
    wi = w_in[l]
    wq = wi[:, :D_ATTN]
    wk = [wi[:, D_ATTN + g * HEAD_DIM:D_ATTN + (g + 1) * HEAD_DIM] for g in range(N_KV_HEADS)]
    wv = [wi[:, D_ATTN + D_KV + g * HEAD_DIM:D_ATTN + D_KV + (g + 1) * HEAD_DIM]
          for g in range(N_KV_HEADS)]
    w_ext = jnp.concatenate([wq, wk[0], wk[0], wk[1], wk[1], wv[0], wv[0], wv[1], wv[1],
                             wi[:, D_ATTN + 2 * D_KV:]], axis=1).astype(BF16)
    q_gain = q_norm_g[l] * (HEAD_DIM ** -0.5 * math.log2(math.e))
    gqk = jnp.concatenate([jnp.tile(q_gain, N_Q_HEADS), jnp.tile(k_norm_g[l], 2 * N_KV_HEADS)])
    seg = jax.scipy.linalg.block_diag(*[jnp.ones((HEAD_DIM, HEAD_DIM), F32)] * (MXU_DIM // HEAD_DIM))
    params = {
        "g_mix": norm_mix_g[l][None, :],
        "w_ext": w_ext,
        "gqk": gqk[None, :],
        "seg": seg.astype(BF16),
        "conv_w": conv_w[l],
        "conv_b": conv_b[l][None, :],
        "wa": jnp.stack([_block_diag_halves(0.5 * lru_wa[l, d]) for d in range(2)]),
        "ba": 0.5 * lru_ba[l],
        "wx": jnp.stack([_block_diag_halves(0.5 * lru_wx[l, d]) for d in range(2)]),
        "bx": 0.5 * lru_bx[l],
        "lam": lru_lambda[l],
        "w_out": w_out[l].astype(BF16),
        "g_mlp": norm_mlp_g[l][None, :],
        "w_up": w_up[l].astype(BF16),
        "w_down": w_down[l].astype(BF16),
        "g_final": norm_final_g[None, :],
    }
    outs = []
    for x in (x_prompt, x_sample):
        outs.append(_layer(x, params, _rope_tables(x.shape[1])))
    return tuple(outs)
```

```python
import functools
import math

import jax
import jax.numpy as jnp
from jax import lax
from jax.experimental import pallas as pl
from jax.experimental.pallas import tpu as pltpu

F32 = jnp.float32
BF16 = jnp.bfloat16

D_MODEL = 1024
HEAD_DIM = 64
N_Q_HEADS = 8
N_KV_HEADS = 2
Q_PER_KV = N_Q_HEADS // N_KV_HEADS
D_ATTN = N_Q_HEADS * HEAD_DIM
D_KV = N_KV_HEADS * HEAD_DIM
D_LRU = 512
N_LRU_BLOCKS = 8
LRU_BLOCK = D_LRU // N_LRU_BLOCKS
LRU_C = 8.0
CONV_W = 4
D_FF = 4 * D_MODEL
GRID_W = 64
ROPE_HALF = HEAD_DIM // 2
ROPE_QUARTER = ROPE_HALF // 2
ROPE_THETA = 10000.0
EPS = 1e-6

LANES = 128
SUBLANES = 8
MXU_DIM = 256

D_PROJ = D_ATTN + 2 * D_KV + 2 * D_LRU
D_QK = D_ATTN + D_KV
D_KDUP = 2 * D_KV

TM_PROJ = 1024
TQ = 512
TK_STEP = 4096
TK_CHUNK = 512
T_LRU = 1024
LRU_CHUNK = T_LRU // SUBLANES + 4
LRU_PAD_ROWS = SUBLANES * LRU_CHUNK
LRU_SLABS = D_LRU // LANES
LRU_UNROLL = 4
TM_MLP = 512
FF_CHUNK = 1024
VMEM_LIMIT = 56 * 1024 * 1024

NEG_BIG = -1e30


def _rms_scale(x):
    return lax.rsqrt(jnp.mean(x * x, axis=-1, keepdims=True) + EPS)


def _proj_kernel(x_ref, g_ref, w_ref, gqk_ref, seg_ref, cos_ref, sin_ref,
                 q_ref, k_ref, v_ref, xbr_ref, ybr_ref):
    x = x_ref[...]
    h = (x * _rms_scale(x) * g_ref[...]).astype(BF16)
    proj = jnp.dot(h, w_ref[...], preferred_element_type=F32)

    cos = cos_ref[...]
    sin = sin_ref[...]
    lane = lax.broadcasted_iota(jnp.int32, cos.shape, 1)
    first_half = (lane & (ROPE_HALF - 1)) < ROPE_QUARTER
    low_half = lane < HEAD_DIM

    def norm_rope(t, seg, gain):
        ssum = jnp.dot((t * t).astype(BF16), seg, preferred_element_type=F32)
        tn = t * lax.rsqrt(ssum * (1.0 / HEAD_DIM) + EPS) * gain
        out = []
        for s in range(t.shape[1] // LANES):
            u = tn[:, s * LANES:(s + 1) * LANES]
            partner = jnp.where(first_half,
                                pltpu.roll(u, LANES - ROPE_QUARTER, 1),
                                pltpu.roll(u, ROPE_QUARTER, 1))
            out.append(u * cos + partner * sin)
        return out

    for c in range(D_ATTN // MXU_DIM):
        cols = slice(c * MXU_DIM, (c + 1) * MXU_DIM)
        for s, r in enumerate(norm_rope(proj[:, cols], seg_ref[...], gqk_ref[:, cols])):
            q_ref[:, c * MXU_DIM + s * LANES:c * MXU_DIM + (s + 1) * LANES] = r.astype(BF16)

    (k01,) = norm_rope(proj[:, D_ATTN:D_ATTN + D_KV], seg_ref[0:LANES, 0:LANES],
                       gqk_ref[:, D_ATTN:D_ATTN + D_KV])
    k10 = pltpu.roll(k01, HEAD_DIM, 1)
    k_ref[:, 0:LANES] = jnp.where(low_half, k01, k10).astype(BF16)
    k_ref[:, LANES:] = jnp.where(low_half, k10, k01).astype(BF16)
    v01 = proj[:, D_ATTN + D_KV:D_ATTN + 2 * D_KV]
    v_ref[:, 0:LANES] = jnp.where(low_half, v01, 1.0).astype(BF16)
    v_ref[:, LANES:] = jnp.where(low_half, pltpu.roll(v01, HEAD_DIM, 1), 1.0).astype(BF16)
    xbr_ref[...] = proj[:, D_ATTN + 2 * D_KV:D_ATTN + 2 * D_KV + D_LRU].astype(BF16)
    ybr_ref[...] = proj[:, D_ATTN + 2 * D_KV + D_LRU:].astype(BF16)


def _proj_call(x2d, g_mix, w_in, gqk, seg, cos_t, sin_t, seq_len):
    n_tok = x2d.shape[0]
    tiles_per_seq = seq_len // TM_PROJ
    const = lambda i: (0, 0)
    tok = lambda i: (i, 0)
    pos = lambda i: (i % tiles_per_seq, 0)
    return pl.pallas_call(
        _proj_kernel,
        grid=(n_tok // TM_PROJ,),
        in_specs=[
            pl.BlockSpec((TM_PROJ, D_MODEL), tok),
            pl.BlockSpec((1, D_MODEL), const),
            pl.BlockSpec((D_MODEL, D_PROJ), const),
            pl.BlockSpec((1, D_QK), const),
            pl.BlockSpec((MXU_DIM, MXU_DIM), const),
            pl.BlockSpec((TM_PROJ, LANES), pos),
            pl.BlockSpec((TM_PROJ, LANES), pos),
        ],
        out_specs=[
            pl.BlockSpec((TM_PROJ, D_ATTN), tok),
            pl.BlockSpec((TM_PROJ, D_KDUP), tok),
            pl.BlockSpec((TM_PROJ, D_KDUP), tok),
            pl.BlockSpec((TM_PROJ, D_LRU), tok),
            pl.BlockSpec((TM_PROJ, D_LRU), tok),
        ],
        out_shape=[
            jax.ShapeDtypeStruct((n_tok, D_ATTN), BF16),
            jax.ShapeDtypeStruct((n_tok, D_KDUP), BF16),
            jax.ShapeDtypeStruct((n_tok, D_KDUP), BF16),
            jax.ShapeDtypeStruct((n_tok, D_LRU), BF16),
            jax.ShapeDtypeStruct((n_tok, D_LRU), BF16),
        ],
        compiler_params=pltpu.CompilerParams(vmem_limit_bytes=VMEM_LIMIT),
        name="proj",
    )(x2d, g_mix, w_in, gqk, seg, cos_t, sin_t)


def _attn_kernel(q_ref, k_ref, v_ref, o_ref, qs_ref, m_ref, acc_ref, *, n_kv):
    j = pl.program_id(3)
    tq, tk_step = q_ref.shape[0], k_ref.shape[0]
    lane = lax.broadcasted_iota(jnp.int32, (tq, LANES), 1)
    low_half = lane < HEAD_DIM

    @pl.when(j == 0)
    def _():
        keep_low = low_half.astype(F32).astype(BF16)
        keep_high = (1.0 - low_half.astype(F32)).astype(BF16)
        for hh in range(Q_PER_KV):
            pair = q_ref[:, (hh // 2) * LANES:(hh // 2 + 1) * LANES]
            qs_ref[hh * tq:(hh + 1) * tq, :] = pair * (keep_low if hh % 2 == 0 else keep_high)
        m_ref[...] = jnp.full(m_ref.shape, NEG_BIG, F32)
        acc_ref[...] = jnp.zeros(acc_ref.shape, F32)

    qs = qs_ref[...]
    m = m_ref[...]
    acc = acc_ref[...]
    for c in range(tk_step // TK_CHUNK):
        rows = slice(c * TK_CHUNK, (c + 1) * TK_CHUNK)
        s = lax.dot_general(qs, k_ref[rows, :], (((1,), (1,)), ((), ())),
                            preferred_element_type=F32)
        m_new = jnp.maximum(m, jnp.max(s, axis=1, keepdims=True))
        alpha = jnp.exp2(m - m_new)
        p = jnp.exp2(s - jnp.tile(m_new, (1, TK_CHUNK // LANES))).astype(BF16)
        acc = alpha * acc + jnp.dot(p, v_ref[rows, :], preferred_element_type=F32)
        m = m_new
    m_ref[...] = m
    acc_ref[...] = acc

    @pl.when(j == n_kv - 1)
    def _():
        a = acc_ref[...]
        swapped = pltpu.roll(a, HEAD_DIM, 1)
        for pr in range(Q_PER_KV // 2):
            ev = slice((2 * pr) * tq, (2 * pr + 1) * tq)
            od = slice((2 * pr + 1) * tq, (2 * pr + 2) * tq)
            num = jnp.where(low_half, a[ev], swapped[od])
            den = jnp.where(low_half, swapped[ev], a[od])
            o_ref[:, pr * LANES:(pr + 1) * LANES] = (num / den).astype(BF16)


def _attn_call(q, kd, va):
    b, s, _ = q.shape
    tq, tk_step = min(s, TQ), min(s, TK_STEP)
    n_q, n_kv = s // tq, s // tk_step
    rows = Q_PER_KV * tq
    return pl.pallas_call(
        functools.partial(_attn_kernel, n_kv=n_kv),
        grid=(b, N_KV_HEADS, n_q, n_kv),
        in_specs=[
            pl.BlockSpec((None, tq, Q_PER_KV * HEAD_DIM), lambda bi, g, i, j: (bi, i, g)),
            pl.BlockSpec((None, tk_step, LANES), lambda bi, g, i, j: (bi, j, g)),
            pl.BlockSpec((None, tk_step, LANES), lambda bi, g, i, j: (bi, j, g)),
        ],
        out_specs=pl.BlockSpec((None, tq, Q_PER_KV * HEAD_DIM), lambda bi, g, i, j: (bi, i, g)),
        out_shape=jax.ShapeDtypeStruct((b, s, D_ATTN), BF16),
        scratch_shapes=[
            pltpu.VMEM((rows, LANES), BF16),
            pltpu.VMEM((rows, LANES), F32),
            pltpu.VMEM((rows, LANES), F32),
        ],
        compiler_params=pltpu.CompilerParams(vmem_limit_bytes=VMEM_LIMIT),
        name="attn",
    )(q, kd, va)


def _lru_gates(x_ref, prev_ref, next_ref, cw_ref, cb_ref, wa_ref, ba_ref, wx_ref, bx_ref,
               lam_ref, a_ref, u_ref, has_prev, has_next):
    t_rows = x_ref.shape[0]
    x = x_ref[...].astype(F32)
    prev = prev_ref[...].astype(F32) * has_prev
    nxt = next_ref[...].astype(F32) * has_next
    row8 = lax.broadcasted_iota(jnp.int32, (SUBLANES, D_LRU), 0)

    def shifted(k):
        if k == 0:
            return x
        rolled = pltpu.roll(x, (-k) % t_rows, 0)
        if k < 0:
            head = jnp.where(row8 < -k, pltpu.roll(prev, -k, 0), rolled[0:SUBLANES])
            return jnp.concatenate([head, rolled[SUBLANES:]], axis=0)
        tail = jnp.where(row8 >= SUBLANES - k, pltpu.roll(nxt, SUBLANES - k, 0), rolled[t_rows - SUBLANES:])
        return jnp.concatenate([rolled[:t_rows - SUBLANES], tail], axis=0)

    xc = cb_ref[...]
    for w in range(CONV_W):
        xc = xc + cw_ref[w:w + 1, :] * shifted(w - 2)
    xcb = xc.astype(BF16)
    half = D_LRU // 2
    r_parts, i_parts = [], []
    for c in range(2):
        xh = xcb[:, c * half:(c + 1) * half]
        r_parts.append(jnp.dot(xh, wa_ref[c], preferred_element_type=F32))
        i_parts.append(jnp.dot(xh, wx_ref[c], preferred_element_type=F32))
    t_r = jnp.tanh(jnp.concatenate(r_parts, axis=1) + ba_ref[...])
    t_i = jnp.tanh(jnp.concatenate(i_parts, axis=1) + bx_ref[...])
    lam = lam_ref[...]
    neg_softplus = -(jnp.maximum(-lam, 0.0) + jnp.log1p(jnp.exp(-jnp.abs(lam))))
    log_a = (0.5 * LRU_C * neg_softplus) * (1.0 + t_r)
    a = jnp.exp(log_a)
    th = jnp.tanh(log_a)
    u = jnp.sqrt(-0.5 * th / (1.0 - th)) * ((1.0 + t_i) * xc)
    for lt in range(LRU_SLABS):
        a_ref[lt, 0:t_rows, :] = a[:, lt * LANES:(lt + 1) * LANES]
        u_ref[lt, 0:t_rows, :] = u[:, lt * LANES:(lt + 1) * LANES]
        a_ref[lt, t_rows:, :] = jnp.ones((LRU_PAD_ROWS - t_rows, LANES), F32)
        u_ref[lt, t_rows:, :] = jnp.zeros((LRU_PAD_ROWS - t_rows, LANES), F32)


def _sublane_shift(x, k, fill, row, reverse):
    if reverse:
        return jnp.where(row < SUBLANES - k, pltpu.roll(x, SUBLANES - k, 0), fill)
    return jnp.where(row >= k, pltpu.roll(x, k, 0), fill)


def _scan_tile(a_ref, u_ref, h_ref, carry_ref, reverse):
    row = lax.broadcasted_iota(jnp.int32, (SUBLANES, LANES), 0)

    def step_rows(t):
        tt = (LRU_CHUNK - 1 - t) if reverse else t
        return pl.ds(tt, SUBLANES, stride=LRU_CHUNK)

    def reduce_step(t, au):
        rows = step_rows(t)
        out_a, out_u = [], []
        for lt in range(LRU_SLABS):
            a_t = a_ref[lt, rows, :]
            out_u.append(a_t * au[1][lt] + u_ref[lt, rows, :])
            out_a.append(a_t * au[0][lt])
        return tuple(out_a), tuple(out_u)

    ones = tuple(jnp.ones((SUBLANES, LANES), F32) for _ in range(LRU_SLABS))
    zeros = tuple(jnp.zeros((SUBLANES, LANES), F32) for _ in range(LRU_SLABS))
    tot_a, tot_u = lax.fori_loop(0, LRU_CHUNK, reduce_step, (ones, zeros), unroll=LRU_UNROLL)

    h_in = []
    for lt in range(LRU_SLABS):
        a, u = tot_a[lt], tot_u[lt]
        for k in (1, 2, 4):
            u = u + a * _sublane_shift(u, k, 0.0, row, reverse)
            a = a * _sublane_shift(a, k, 1.0, row, reverse)
        carry = carry_ref[lt]
        h_out = u + a * carry
        h_in.append(_sublane_shift(h_out, 1, carry, row, reverse))
        last = h_out[0:1, :] if reverse else h_out[SUBLANES - 1:SUBLANES, :]
        carry_ref[lt] = jnp.broadcast_to(last, (SUBLANES, LANES))

    def scan_step(t, hs):
        rows = step_rows(t)
        out = []
        for lt in range(LRU_SLABS):
            h = a_ref[lt, rows, :] * hs[lt] + u_ref[lt, rows, :]
            h_ref[lt, rows, :] = h
            out.append(h)
        return tuple(out)

    lax.fori_loop(0, LRU_CHUNK, scan_step, tuple(h_in), unroll=LRU_UNROLL)


def _lru_fwd_kernel(x_ref, prev_ref, next_ref, cw_ref, cb_ref, wa_ref, ba_ref, wx_ref, bx_ref, lam_ref,
                    hf_ref, a_ref, u_ref, h_ref, carry_ref, *, n_tiles):
    i = pl.program_id(1)

    @pl.when(i == 0)
    def _():
        carry_ref[...] = jnp.zeros(carry_ref.shape, F32)

    has_prev = (i > 0).astype(F32)
    has_next = (i < n_tiles - 1).astype(F32)
    _lru_gates(x_ref, prev_ref, next_ref, cw_ref, cb_ref, wa_ref, ba_ref, wx_ref, bx_ref,
               lam_ref, a_ref, u_ref, has_prev, has_next)
    _scan_tile(a_ref, u_ref, h_ref, carry_ref, reverse=False)
    for lt in range(LRU_SLABS):
        hf_ref[:, lt * LANES:(lt + 1) * LANES] = h_ref[lt, 0:T_LRU, :]


def _lru_bwd_kernel(x_ref, prev_ref, next_ref, y_ref, hf_ref, cw_ref, cb_ref, wa_ref, ba_ref, wx_ref,
                    bx_ref, lam_ref, o_ref, a_ref, u_ref, h_ref, carry_ref, *, n_tiles):
    i = n_tiles - 1 - pl.program_id(1)

    @pl.when(i == n_tiles - 1)
    def _():
        carry_ref[...] = jnp.zeros(carry_ref.shape, F32)

    has_prev = (i > 0).astype(F32)
    has_next = (i < n_tiles - 1).astype(F32)
    _lru_gates(x_ref, prev_ref, next_ref, cw_ref, cb_ref, wa_ref, ba_ref, wx_ref, bx_ref,
               lam_ref, a_ref, u_ref, has_prev, has_next)
    _scan_tile(a_ref, u_ref, h_ref, carry_ref, reverse=True)
    for lt in range(LRU_SLABS):
        cols = slice(lt * LANES, (lt + 1) * LANES)
        y = y_ref[:, cols].astype(F32)
        gelu = 0.5 * y * (1.0 + jnp.tanh(math.sqrt(2.0 / math.pi) * (y + 0.044715 * (y * y * y))))
        o_ref[:, cols] = ((hf_ref[:, cols] + h_ref[lt, 0:T_LRU, :]) * gelu).astype(BF16)


def _lru_specs(n_tiles, reverse):
    blocks_per_tile = T_LRU // SUBLANES
    n_blocks = n_tiles * blocks_per_tile
    tile_of = (lambda i: n_tiles - 1 - i) if reverse else (lambda i: i)
    tile = lambda b, i: (b, tile_of(i), 0)
    prev = lambda b, i: (b, jnp.maximum(tile_of(i) * blocks_per_tile - 1, 0), 0)
    nxt = lambda b, i: (b, jnp.minimum((tile_of(i) + 1) * blocks_per_tile, n_blocks - 1), 0)
    c2 = lambda b, i: (0, 0)
    c3 = lambda b, i: (0, 0, 0)
    half = D_LRU // 2
    x_specs = [
        pl.BlockSpec((None, T_LRU, D_LRU), tile),
        pl.BlockSpec((None, SUBLANES, D_LRU), prev),
        pl.BlockSpec((None, SUBLANES, D_LRU), nxt),
    ]
    w_specs = [
        pl.BlockSpec((CONV_W, D_LRU), c2),
        pl.BlockSpec((1, D_LRU), c2),
        pl.BlockSpec((2, half, half), c3),
        pl.BlockSpec((1, D_LRU), c2),
        pl.BlockSpec((2, half, half), c3),
        pl.BlockSpec((1, D_LRU), c2),
        pl.BlockSpec((1, D_LRU), c2),
    ]
    return x_specs, w_specs, tile


def _lru_call(xbr, ybr, conv_w, conv_b, wa, ba, wx, bx, lam):
    b, s, _ = xbr.shape
    n_tiles = s // T_LRU
    slab = pltpu.VMEM((LRU_SLABS, LRU_PAD_ROWS, LANES), F32)
    scratch = [slab, slab, slab]
    carry = pltpu.VMEM((LRU_SLABS, SUBLANES, LANES), F32)
    params = pltpu.CompilerParams(vmem_limit_bytes=VMEM_LIMIT)

    x_specs, w_specs, tile = _lru_specs(n_tiles, reverse=False)
    h_fwd = pl.pallas_call(
        functools.partial(_lru_fwd_kernel, n_tiles=n_tiles),
        grid=(b, n_tiles),
        in_specs=x_specs + w_specs,
        out_specs=pl.BlockSpec((None, T_LRU, D_LRU), tile),
        out_shape=jax.ShapeDtypeStruct((b, s, D_LRU), F32),
        scratch_shapes=scratch + [carry],
        compiler_params=params,
        name="lru_fwd",
    )(xbr, xbr, xbr, conv_w, conv_b, wa[0], ba[0:1], wx[0], bx[0:1], lam[0:1])

    x_specs, w_specs, tile = _lru_specs(n_tiles, reverse=True)
    return pl.pallas_call(
        functools.partial(_lru_bwd_kernel, n_tiles=n_tiles),
        grid=(b, n_tiles),
        in_specs=x_specs + [pl.BlockSpec((None, T_LRU, D_LRU), tile),
                            pl.BlockSpec((None, T_LRU, D_LRU), tile)] + w_specs,
        out_specs=pl.BlockSpec((None, T_LRU, D_LRU), tile),
        out_shape=jax.ShapeDtypeStruct((b, s, D_LRU), BF16),
        scratch_shapes=scratch + [carry],
        compiler_params=params,
        name="lru_bwd",
    )(xbr, xbr, xbr, ybr, h_fwd, conv_w, conv_b, wa[1], ba[1:2], wx[1], bx[1:2], lam[1:2])


def _mlp_kernel(x_ref, attn_ref, rec_ref, wo_ref, g2_ref, wup_ref, wdn_ref, gf_ref, o_ref):
    x1 = (x_ref[...]
          + jnp.dot(attn_ref[...], wo_ref[0:D_ATTN, :], preferred_element_type=F32)
          + jnp.dot(rec_ref[...], wo_ref[D_ATTN:, :], preferred_element_type=F32))
    h = (x1 * _rms_scale(x1) * g2_ref[...]).astype(BF16)
    mlp = None
    for c in range(D_FF // FF_CHUNK):
        up = jnp.dot(h, wup_ref[:, c * FF_CHUNK:(c + 1) * FF_CHUNK], preferred_element_type=F32)
        act = jnp.square(jnp.maximum(up, 0.0)).astype(BF16)
        down = jnp.dot(act, wdn_ref[c * FF_CHUNK:(c + 1) * FF_CHUNK, :], preferred_element_type=F32)
        mlp = down if mlp is None else mlp + down
    x2 = x1 + mlp
    o_ref[...] = x2 * _rms_scale(x2) * gf_ref[...]


def _mlp_call(x2d, attn2d, rec2d, w_out, g_mlp, w_up, w_down, g_final):
    n_tok = x2d.shape[0]
    const = lambda i: (0, 0)
    tok = lambda i: (i, 0)
    resident = pl.Buffered(1)
    return pl.pallas_call(
        _mlp_kernel,
        grid=(n_tok // TM_MLP,),
        in_specs=[
            pl.BlockSpec((TM_MLP, D_MODEL), tok),
            pl.BlockSpec((TM_MLP, D_ATTN), tok),
            pl.BlockSpec((TM_MLP, D_LRU), tok),
            pl.BlockSpec((D_MODEL, D_MODEL), const, pipeline_mode=resident),
            pl.BlockSpec((1, D_MODEL), const),
            pl.BlockSpec((D_MODEL, D_FF), const, pipeline_mode=resident),
            pl.BlockSpec((D_FF, D_MODEL), const, pipeline_mode=resident),
            pl.BlockSpec((1, D_MODEL), const),
        ],
        out_specs=pl.BlockSpec((TM_MLP, D_MODEL), tok),
        out_shape=jax.ShapeDtypeStruct((n_tok, D_MODEL), F32),
        compiler_params=pltpu.CompilerParams(vmem_limit_bytes=VMEM_LIMIT),
        name="mlp",
    )(x2d, attn2d, rec2d, w_out, g_mlp, w_up, w_down, g_final)


def _rope_tables(seq_len):
    rows = seq_len // GRID_W
    row_ids = jnp.repeat(jnp.arange(rows), GRID_W).astype(F32)
    col_ids = jnp.tile(jnp.arange(GRID_W), rows).astype(F32)
    inv_freq = ROPE_THETA ** (-jnp.arange(0, ROPE_HALF, 2, dtype=F32) / ROPE_HALF)
    ang_r = row_ids[:, None] * inv_freq
    ang_c = col_ids[:, None] * inv_freq
    cos = jnp.concatenate([jnp.cos(ang_r)] * 2 + [jnp.cos(ang_c)] * 2, axis=-1)
    sin = jnp.concatenate([-jnp.sin(ang_r), jnp.sin(ang_r), -jnp.sin(ang_c), jnp.sin(ang_c)], axis=-1)
    return jnp.tile(cos, (1, LANES // HEAD_DIM)), jnp.tile(sin, (1, LANES // HEAD_DIM))


def _block_diag_halves(w):
    per_half = N_LRU_BLOCKS // 2
    halves = [jax.scipy.linalg.block_diag(*[w[c * per_half + n] for n in range(per_half)])
              for c in range(2)]
    return jnp.stack(halves).astype(BF16)


def _layer(x, params, seq_tables):
    b, s, _ = x.shape
    x2d = x.reshape(b * s, D_MODEL)
    q, kd, va, xbr, ybr = _proj_call(x2d, params["g_mix"], params["w_in"], params["gqk"],
                                     params["seg"], seq_tables[0], seq_tables[1], s)
    attn = _attn_call(q.reshape(b, s, D_ATTN), kd.reshape(b, s, D_KDUP), va.reshape(b, s, D_KDUP))
    rec = _lru_call(xbr.reshape(b, s, D_LRU), ybr.reshape(b, s, D_LRU), params["conv_w"],
                    params["conv_b"], params["wa"], params["ba"], params["wx"], params["bx"],
                    params["lam"])
    y = _mlp_call(x2d, attn.reshape(b * s, D_ATTN), rec.reshape(b * s, D_LRU), params["w_out"],
                  params["g_mlp"], params["w_up"], params["w_down"], params["g_final"])
    return y.reshape(b, s, D_MODEL)


def kernel(x_prompt, x_sample, norm_mix_g, w_in, q_norm_g, k_norm_g, conv_w, conv_b, lru_wa, lru_ba,
           lru_wx, lru_bx, lru_lambda, w_out, norm_mlp_g, w_up, w_down, norm_final_g):
    l = 0
    q_gain = q_norm_g[l] * (HEAD_DIM ** -0.5 * math.log2(math.e))
    gqk = jnp.concatenate([jnp.tile(q_gain, N_Q_HEADS), jnp.tile(k_norm_g[l], N_KV_HEADS)])
    seg = jax.scipy.linalg.block_diag(*[jnp.ones((HEAD_DIM, HEAD_DIM), F32)] * (MXU_DIM // HEAD_DIM))
    params = {
        "g_mix": norm_mix_g[l][None, :],
        "w_in": w_in[l].astype(BF16),
        "gqk": gqk[None, :],
        "seg": seg.astype(BF16),
        "conv_w": conv_w[l],
        "conv_b": conv_b[l][None, :],
        "wa": jnp.stack([_block_diag_halves(0.5 * lru_wa[l, d]) for d in range(2)]),
        "ba": 0.5 * lru_ba[l],
        "wx": jnp.stack([_block_diag_halves(0.5 * lru_wx[l, d]) for d in range(2)]),
        "bx": 0.5 * lru_bx[l],
        "lam": lru_lambda[l],
        "w_out": w_out[l].astype(BF16),
        "g_mlp": norm_mlp_g[l][None, :],
        "w_up": w_up[l].astype(BF16),
        "w_down": w_down[l].astype(BF16),
        "g_final": norm_final_g[None, :],
    }
    outs = []
    for x in (x_prompt, x_sample):
        outs.append(_layer(x, params, _rope_tables(x.shape[1])))
    return tuple(outs)
```

```python
import functools
import math

import jax
import jax.numpy as jnp
from jax import lax
from jax.experimental import pallas as pl
from jax.experimental.pallas import tpu as pltpu

F32 = jnp.float32
BF16 = jnp.bfloat16

D_MODEL = 1024
HEAD_DIM = 64
N_Q_HEADS = 8
N_KV_HEADS = 2
Q_PER_KV = N_Q_HEADS // N_KV_HEADS
D_ATTN = N_Q_HEADS * HEAD_DIM
D_KV = N_KV_HEADS * HEAD_DIM
D_LRU = 512
N_LRU_BLOCKS = 8
LRU_BLOCK = D_LRU // N_LRU_BLOCKS
LRU_C = 8.0
CONV_W = 4
D_FF = 4 * D_MODEL
GRID_W = 64
ROPE_HALF = HEAD_DIM // 2
ROPE_QUARTER = ROPE_HALF // 2
ROPE_THETA = 10000.0
EPS = 1e-6

LANES = 128
SUBLANES = 8
MXU_DIM = 256

D_PROJ = D_ATTN + 2 * D_KV + 2 * D_LRU
D_QK = D_ATTN + D_KV
D_KDUP = 2 * D_KV

TM_PROJ = 1024
TQ = 512
TK_STEP = 4096
TK_CHUNK = 512
T_LRU = 1024
LRU_CHUNK = T_LRU // SUBLANES + 4
LRU_PAD_ROWS = SUBLANES * LRU_CHUNK
LRU_SLABS = D_LRU // LANES
LRU_UNROLL = 4
TM_MLP = 512
FF_CHUNK = 1024
VMEM_LIMIT = 56 * 1024 * 1024

NEG_BIG = -1e30
MAX_SHIFTED_RANGE = 100.0
SCORE_BOUND_SLACK = 1.0 + 2.0 ** -6


def _rms_scale(x):
    return lax.rsqrt(jnp.mean(x * x, axis=-1, keepdims=True) + EPS)


def _proj_kernel(x_ref, g_ref, w_ref, gqk_ref, seg_ref, cos_ref, sin_ref,
                 q_ref, k_ref, v_ref, xbr_ref, ybr_ref):
    x = x_ref[...]
    h = (x * _rms_scale(x) * g_ref[...]).astype(BF16)
    proj = jnp.dot(h, w_ref[...], preferred_element_type=F32)

    cos = cos_ref[...]
    sin = sin_ref[...]
    lane = lax.broadcasted_iota(jnp.int32, cos.shape, 1)
    first_half = (lane & (ROPE_HALF - 1)) < ROPE_QUARTER
    low_half = lane < HEAD_DIM

    def norm_rope(t, seg, gain):
        ssum = jnp.dot((t * t).astype(BF16), seg, preferred_element_type=F32)
        tn = t * lax.rsqrt(ssum * (1.0 / HEAD_DIM) + EPS) * gain
        out = []
        for s in range(t.shape[1] // LANES):
            u = tn[:, s * LANES:(s + 1) * LANES]
            partner = jnp.where(first_half,
                                pltpu.roll(u, LANES - ROPE_QUARTER, 1),
                                pltpu.roll(u, ROPE_QUARTER, 1))
            out.append(u * cos + partner * sin)
        return out

    for c in range(D_ATTN // MXU_DIM):
        cols = slice(c * MXU_DIM, (c + 1) * MXU_DIM)
        for s, r in enumerate(norm_rope(proj[:, cols], seg_ref[...], gqk_ref[:, cols])):
            q_ref[:, c * MXU_DIM + s * LANES:c * MXU_DIM + (s + 1) * LANES] = r.astype(BF16)

    (k01,) = norm_rope(proj[:, D_ATTN:D_ATTN + D_KV], seg_ref[0:LANES, 0:LANES],
                       gqk_ref[:, D_ATTN:D_ATTN + D_KV])
    unit_lane = (lane == HEAD_DIM).astype(F32)
    k_ref[:, 0:LANES] = jnp.where(low_half, k01, unit_lane).astype(BF16)
    k_ref[:, LANES:] = jnp.where(low_half, pltpu.roll(k01, HEAD_DIM, 1), unit_lane).astype(BF16)
    v01 = proj[:, D_ATTN + D_KV:D_ATTN + 2 * D_KV]
    v_ref[:, 0:LANES] = jnp.where(low_half, v01, 1.0).astype(BF16)
    v_ref[:, LANES:] = jnp.where(low_half, pltpu.roll(v01, HEAD_DIM, 1), 1.0).astype(BF16)
    xbr_ref[...] = proj[:, D_ATTN + 2 * D_KV:D_ATTN + 2 * D_KV + D_LRU].astype(BF16)
    ybr_ref[...] = proj[:, D_ATTN + 2 * D_KV + D_LRU:].astype(BF16)


def _proj_call(x2d, g_mix, w_in, gqk, seg, cos_t, sin_t, seq_len):
    n_tok = x2d.shape[0]
    tiles_per_seq = seq_len // TM_PROJ
    const = lambda i: (0, 0)
    tok = lambda i: (i, 0)
    pos = lambda i: (i % tiles_per_seq, 0)
    return pl.pallas_call(
        _proj_kernel,
        grid=(n_tok // TM_PROJ,),
        in_specs=[
            pl.BlockSpec((TM_PROJ, D_MODEL), tok),
            pl.BlockSpec((1, D_MODEL), const),
            pl.BlockSpec((D_MODEL, D_PROJ), const),
            pl.BlockSpec((1, D_QK), const),
            pl.BlockSpec((MXU_DIM, MXU_DIM), const),
            pl.BlockSpec((TM_PROJ, LANES), pos),
            pl.BlockSpec((TM_PROJ, LANES), pos),
        ],
        out_specs=[
            pl.BlockSpec((TM_PROJ, D_ATTN), tok),
            pl.BlockSpec((TM_PROJ, D_KDUP), tok),
            pl.BlockSpec((TM_PROJ, D_KDUP), tok),
            pl.BlockSpec((TM_PROJ, D_LRU), tok),
            pl.BlockSpec((TM_PROJ, D_LRU), tok),
        ],
        out_shape=[
            jax.ShapeDtypeStruct((n_tok, D_ATTN), BF16),
            jax.ShapeDtypeStruct((n_tok, D_KDUP), BF16),
            jax.ShapeDtypeStruct((n_tok, D_KDUP), BF16),
            jax.ShapeDtypeStruct((n_tok, D_LRU), BF16),
            jax.ShapeDtypeStruct((n_tok, D_LRU), BF16),
        ],
        compiler_params=pltpu.CompilerParams(vmem_limit_bytes=VMEM_LIMIT),
        name="proj",
    )(x2d, g_mix, w_in, gqk, seg, cos_t, sin_t)


def _attn_kernel(flag_ref, q_ref, k_ref, v_ref, shift_ref, o_ref, qs_ref, acc_ref, m_ref, *, n_kv):
    j = pl.program_id(3)
    bounded = flag_ref[0] != 0
    tq, tk_step = q_ref.shape[0], k_ref.shape[0]
    lane = lax.broadcasted_iota(jnp.int32, (tq, LANES), 1)
    low_half = lane < HEAD_DIM
    chunks = [slice(c * TK_CHUNK, (c + 1) * TK_CHUNK) for c in range(tk_step // TK_CHUNK)]

    def stack_queries():
        for hh in range(Q_PER_KV):
            pair = q_ref[:, (hh // 2) * LANES:(hh // 2 + 1) * LANES].astype(F32)
            if hh % 2:
                pair = pltpu.roll(pair, HEAD_DIM, 1)
            qs_ref[hh * tq:(hh + 1) * tq, :] = jnp.where(low_half, pair, shift_ref[...]).astype(BF16)

    def scores(rows):
        return lax.dot_general(qs_ref[...], k_ref[rows, :], (((1,), (1,)), ((), ())),
                               preferred_element_type=F32)

    def bounded_chunks(first_assigns):
        for c, rows in enumerate(chunks):
            p = jnp.exp2(scores(rows)).astype(BF16)
            part = jnp.dot(p, v_ref[rows, :], preferred_element_type=F32)
            if first_assigns and c == 0:
                acc_ref[...] = part
            else:
                acc_ref[...] += part

    def online_chunks():
        m = m_ref[...]
        acc = acc_ref[...]
        for rows in chunks:
            s = scores(rows)
            m_new = jnp.maximum(m, jnp.max(s, axis=1, keepdims=True))
            alpha = jnp.exp2(m - m_new)
            p = jnp.exp2(s - jnp.tile(m_new, (1, TK_CHUNK // LANES))).astype(BF16)
            acc = alpha * acc + jnp.dot(p, v_ref[rows, :], preferred_element_type=F32)
            m = m_new
        m_ref[...] = m
        acc_ref[...] = acc

    def init_online():
        acc_ref[...] = jnp.zeros(acc_ref.shape, F32)
        m_ref[...] = jnp.full(m_ref.shape, NEG_BIG, F32)

    def finalize():
        a = acc_ref[...]
        swapped = pltpu.roll(a, HEAD_DIM, 1)
        for pr in range(Q_PER_KV // 2):
            ev = slice((2 * pr) * tq, (2 * pr + 1) * tq)
            od = slice((2 * pr + 1) * tq, (2 * pr + 2) * tq)
            num = jnp.where(low_half, a[ev], swapped[od])
            den = jnp.where(low_half, swapped[ev], a[od])
            o_ref[:, pr * LANES:(pr + 1) * LANES] = (num / den).astype(BF16)

    if n_kv == 1:
        @pl.when(bounded)
        def _():
            stack_queries()
            bounded_chunks(first_assigns=True)
            finalize()

        @pl.when(jnp.logical_not(bounded))
        def _():
            stack_queries()
            init_online()
            online_chunks()
            finalize()
    else:
        @pl.when(j == 0)
        def _():
            stack_queries()
            init_online()

        @pl.when(bounded)
        def _():
            bounded_chunks(first_assigns=False)

        @pl.when(jnp.logical_not(bounded))
        def _():
            online_chunks()

        @pl.when(j == n_kv - 1)
        def _():
            finalize()


def _attention(q, ka, va, score_bound):
    b, s, _ = q.shape
    tq, tk_step = min(s, TQ), min(s, TK_STEP)
    n_q, n_kv = s // tq, s // tk_step
    rows = Q_PER_KV * tq
    use_bounded = 2.0 * score_bound <= MAX_SHIFTED_RANGE
    flag = use_bounded.astype(jnp.int32).reshape(1)
    shift = jnp.zeros((1, LANES), F32).at[0, HEAD_DIM].set(jnp.where(use_bounded, -score_bound, 0.0))
    return pl.pallas_call(
        functools.partial(_attn_kernel, n_kv=n_kv),
        grid=(b, N_KV_HEADS, n_q, n_kv),
        in_specs=[
            pl.BlockSpec(memory_space=pltpu.SMEM),
            pl.BlockSpec((None, tq, Q_PER_KV * HEAD_DIM), lambda bi, g, i, j: (bi, i, g)),
            pl.BlockSpec((None, tk_step, LANES), lambda bi, g, i, j: (bi, j, g)),
            pl.BlockSpec((None, tk_step, LANES), lambda bi, g, i, j: (bi, j, g)),
            pl.BlockSpec((1, LANES), lambda bi, g, i, j: (0, 0)),
        ],
        out_specs=pl.BlockSpec((None, tq, Q_PER_KV * HEAD_DIM), lambda bi, g, i, j: (bi, i, g)),
        out_shape=jax.ShapeDtypeStruct((b, s, D_ATTN), BF16),
        scratch_shapes=[
            pltpu.VMEM((rows, LANES), BF16),
            pltpu.VMEM((rows, LANES), F32),
            pltpu.VMEM((rows, LANES), F32),
        ],
        compiler_params=pltpu.CompilerParams(vmem_limit_bytes=VMEM_LIMIT),
        name="attn",
    )(flag, q, ka, va, shift)


def _lru_gates(x_ref, prev_ref, next_ref, cw_ref, cb_ref, wa_ref, ba_ref, wx_ref, bx_ref,
               lam_ref, a_ref, u_ref, has_prev, has_next):
    t_rows = x_ref.shape[0]
    x = x_ref[...].astype(F32)
    prev = prev_ref[...].astype(F32) * has_prev
    nxt = next_ref[...].astype(F32) * has_next
    row8 = lax.broadcasted_iota(jnp.int32, (SUBLANES, D_LRU), 0)

    def shifted(k):
        if k == 0:
            return x
        rolled = pltpu.roll(x, (-k) % t_rows, 0)
        if k < 0:
            head = jnp.where(row8 < -k, pltpu.roll(prev, -k, 0), rolled[0:SUBLANES])
            return jnp.concatenate([head, rolled[SUBLANES:]], axis=0)
        tail = jnp.where(row8 >= SUBLANES - k, pltpu.roll(nxt, SUBLANES - k, 0), rolled[t_rows - SUBLANES:])
        return jnp.concatenate([rolled[:t_rows - SUBLANES], tail], axis=0)

    xc = cb_ref[...]
    for w in range(CONV_W):
        xc = xc + cw_ref[w:w + 1, :] * shifted(w - 2)
    xcb = xc.astype(BF16)
    half = D_LRU // 2
    r_parts, i_parts = [], []
    for c in range(2):
        xh = xcb[:, c * half:(c + 1) * half]
        r_parts.append(jnp.dot(xh, wa_ref[c], preferred_element_type=F32))
        i_parts.append(jnp.dot(xh, wx_ref[c], preferred_element_type=F32))
    t_r = jnp.tanh(jnp.concatenate(r_parts, axis=1) + ba_ref[...])
    t_i = jnp.tanh(jnp.concatenate(i_parts, axis=1) + bx_ref[...])
    lam = lam_ref[...]
    neg_softplus = -(jnp.maximum(-lam, 0.0) + jnp.log1p(jnp.exp(-jnp.abs(lam))))
    log_a = (0.5 * LRU_C * neg_softplus) * (1.0 + t_r)
    a = jnp.exp(log_a)
    th = jnp.tanh(log_a)
    u = jnp.sqrt(-0.5 * th / (1.0 - th)) * ((1.0 + t_i) * xc)
    for lt in range(LRU_SLABS):
        a_ref[lt, 0:t_rows, :] = a[:, lt * LANES:(lt + 1) * LANES]
        u_ref[lt, 0:t_rows, :] = u[:, lt * LANES:(lt + 1) * LANES]
        a_ref[lt, t_rows:, :] = jnp.ones((LRU_PAD_ROWS - t_rows, LANES), F32)
        u_ref[lt, t_rows:, :] = jnp.zeros((LRU_PAD_ROWS - t_rows, LANES), F32)


def _sublane_shift(x, k, fill, row, reverse):
    if reverse:
        return jnp.where(row < SUBLANES - k, pltpu.roll(x, SUBLANES - k, 0), fill)
    return jnp.where(row >= k, pltpu.roll(x, k, 0), fill)


def _scan_tile(a_ref, u_ref, h_ref, carry_ref, reverse):
    row = lax.broadcasted_iota(jnp.int32, (SUBLANES, LANES), 0)

    def step_rows(t):
        tt = (LRU_CHUNK - 1 - t) if reverse else t
        return pl.ds(tt, SUBLANES, stride=LRU_CHUNK)

    def reduce_step(t, au):
        rows = step_rows(t)
        out_a, out_u = [], []
        for lt in range(LRU_SLABS):
            a_t = a_ref[lt, rows, :]
            out_u.append(a_t * au[1][lt] + u_ref[lt, rows, :])
            out_a.append(a_t * au[0][lt])
        return tuple(out_a), tuple(out_u)

    ones = tuple(jnp.ones((SUBLANES, LANES), F32) for _ in range(LRU_SLABS))
    zeros = tuple(jnp.zeros((SUBLANES, LANES), F32) for _ in range(LRU_SLABS))
    tot_a, tot_u = lax.fori_loop(0, LRU_CHUNK, reduce_step, (ones, zeros), unroll=LRU_UNROLL)

    h_in = []
    for lt in range(LRU_SLABS):
        a, u = tot_a[lt], tot_u[lt]
        for k in (1, 2, 4):
            u = u + a * _sublane_shift(u, k, 0.0, row, reverse)
            a = a * _sublane_shift(a, k, 1.0, row, reverse)
        carry = carry_ref[lt]
        h_out = u + a * carry
        h_in.append(_sublane_shift(h_out, 1, carry, row, reverse))
        last = h_out[0:1, :] if reverse else h_out[SUBLANES - 1:SUBLANES, :]
        carry_ref[lt] = jnp.broadcast_to(last, (SUBLANES, LANES))

    def scan_step(t, hs):
        rows = step_rows(t)
        out = []
        for lt in range(LRU_SLABS):
            h = a_ref[lt, rows, :] * hs[lt] + u_ref[lt, rows, :]
            h_ref[lt, rows, :] = h
            out.append(h)
        return tuple(out)

    lax.fori_loop(0, LRU_CHUNK, scan_step, tuple(h_in), unroll=LRU_UNROLL)


def _lru_fwd_kernel(x_ref, prev_ref, next_ref, cw_ref, cb_ref, wa_ref, ba_ref, wx_ref, bx_ref, lam_ref,
                    hf_ref, a_ref, u_ref, h_ref, carry_ref, *, n_tiles):
    i = pl.program_id(1)

    @pl.when(i == 0)
    def _():
        carry_ref[...] = jnp.zeros(carry_ref.shape, F32)

    has_prev = (i > 0).astype(F32)
    has_next = (i < n_tiles - 1).astype(F32)
    _lru_gates(x_ref, prev_ref, next_ref, cw_ref, cb_ref, wa_ref, ba_ref, wx_ref, bx_ref,
               lam_ref, a_ref, u_ref, has_prev, has_next)
    _scan_tile(a_ref, u_ref, h_ref, carry_ref, reverse=False)
    for lt in range(LRU_SLABS):
        hf_ref[:, lt * LANES:(lt + 1) * LANES] = h_ref[lt, 0:T_LRU, :]


def _lru_bwd_kernel(x_ref, prev_ref, next_ref, y_ref, hf_ref, cw_ref, cb_ref, wa_ref, ba_ref, wx_ref,
                    bx_ref, lam_ref, o_ref, a_ref, u_ref, h_ref, carry_ref, *, n_tiles):
    i = n_tiles - 1 - pl.program_id(1)

    @pl.when(i == n_tiles - 1)
    def _():
        carry_ref[...] = jnp.zeros(carry_ref.shape, F32)

    has_prev = (i > 0).astype(F32)
    has_next = (i < n_tiles - 1).astype(F32)
    _lru_gates(x_ref, prev_ref, next_ref, cw_ref, cb_ref, wa_ref, ba_ref, wx_ref, bx_ref,
               lam_ref, a_ref, u_ref, has_prev, has_next)
    _scan_tile(a_ref, u_ref, h_ref, carry_ref, reverse=True)
    for lt in range(LRU_SLABS):
        cols = slice(lt * LANES, (lt + 1) * LANES)
        y = y_ref[:, cols].astype(F32)
        gelu = 0.5 * y * (1.0 + jnp.tanh(math.sqrt(2.0 / math.pi) * (y + 0.044715 * (y * y * y))))
        o_ref[:, cols] = ((hf_ref[:, cols] + h_ref[lt, 0:T_LRU, :]) * gelu).astype(BF16)


def _lru_specs(n_tiles, reverse):
    blocks_per_tile = T_LRU // SUBLANES
    n_blocks = n_tiles * blocks_per_tile
    tile_of = (lambda i: n_tiles - 1 - i) if reverse else (lambda i: i)
    tile = lambda b, i: (b, tile_of(i), 0)
    prev = lambda b, i: (b, jnp.maximum(tile_of(i) * blocks_per_tile - 1, 0), 0)
    nxt = lambda b, i: (b, jnp.minimum((tile_of(i) + 1) * blocks_per_tile, n_blocks - 1), 0)
    c2 = lambda b, i: (0, 0)
    c3 = lambda b, i: (0, 0, 0)
    half = D_LRU // 2
    x_specs = [
        pl.BlockSpec((None, T_LRU, D_LRU), tile),
        pl.BlockSpec((None, SUBLANES, D_LRU), prev),
        pl.BlockSpec((None, SUBLANES, D_LRU), nxt),
    ]
    w_specs = [
        pl.BlockSpec((CONV_W, D_LRU), c2),
        pl.BlockSpec((1, D_LRU), c2),
        pl.BlockSpec((2, half, half), c3),
        pl.BlockSpec((1, D_LRU), c2),
        pl.BlockSpec((2, half, half), c3),
        pl.BlockSpec((1, D_LRU), c2),
        pl.BlockSpec((1, D_LRU), c2),
    ]
    return x_specs, w_specs, tile


def _lru_call(xbr, ybr, conv_w, conv_b, wa, ba, wx, bx, lam):
    b, s, _ = xbr.shape
    n_tiles = s // T_LRU
    slab = pltpu.VMEM((LRU_SLABS, LRU_PAD_ROWS, LANES), F32)
    scratch = [slab, slab, slab]
    carry = pltpu.VMEM((LRU_SLABS, SUBLANES, LANES), F32)
    params = pltpu.CompilerParams(vmem_limit_bytes=VMEM_LIMIT)

    x_specs, w_specs, tile = _lru_specs(n_tiles, reverse=False)
    h_fwd = pl.pallas_call(
        functools.partial(_lru_fwd_kernel, n_tiles=n_tiles),
        grid=(b, n_tiles),
        in_specs=x_specs + w_specs,
        out_specs=pl.BlockSpec((None, T_LRU, D_LRU), tile),
        out_shape=jax.ShapeDtypeStruct((b, s, D_LRU), F32),
        scratch_shapes=scratch + [carry],
        compiler_params=params,
        name="lru_fwd",
    )(xbr, xbr, xbr, conv_w, conv_b, wa[0], ba[0:1], wx[0], bx[0:1], lam[0:1])

    x_specs, w_specs, tile = _lru_specs(n_tiles, reverse=True)
    return pl.pallas_call(
        functools.partial(_lru_bwd_kernel, n_tiles=n_tiles),
        grid=(b, n_tiles),
        in_specs=x_specs + [pl.BlockSpec((None, T_LRU, D_LRU), tile),
                            pl.BlockSpec((None, T_LRU, D_LRU), tile)] + w_specs,
        out_specs=pl.BlockSpec((None, T_LRU, D_LRU), tile),
        out_shape=jax.ShapeDtypeStruct((b, s, D_LRU), BF16),
        scratch_shapes=scratch + [carry],
        compiler_params=params,
        name="lru_bwd",
    )(xbr, xbr, xbr, ybr, h_fwd, conv_w, conv_b, wa[1], ba[1:2], wx[1], bx[1:2], lam[1:2])


def _mlp_kernel(x_ref, attn_ref, rec_ref, wo_ref, g2_ref, wup_ref, wdn_ref, gf_ref, o_ref):
    x1 = (x_ref[...]
          + jnp.dot(attn_ref[...], wo_ref[0:D_ATTN, :], preferred_element_type=F32)
          + jnp.dot(rec_ref[...], wo_ref[D_ATTN:, :], preferred_element_type=F32))
    h = (x1 * _rms_scale(x1) * g2_ref[...]).astype(BF16)
    mlp = None
    for c in range(D_FF // FF_CHUNK):
        up = jnp.dot(h, wup_ref[:, c * FF_CHUNK:(c + 1) * FF_CHUNK], preferred_element_type=F32)
        act = jnp.square(jnp.maximum(up, 0.0)).astype(BF16)
        down = jnp.dot(act, wdn_ref[c * FF_CHUNK:(c + 1) * FF_CHUNK, :], preferred_element_type=F32)
        mlp = down if mlp is None else mlp + down
    x2 = x1 + mlp
    o_ref[...] = x2 * _rms_scale(x2) * gf_ref[...]


def _mlp_call(x2d, attn2d, rec2d, w_out, g_mlp, w_up, w_down, g_final):
    n_tok = x2d.shape[0]
    const = lambda i: (0, 0)
    tok = lambda i: (i, 0)
    resident = pl.Buffered(1)
    return pl.pallas_call(
        _mlp_kernel,
        grid=(n_tok // TM_MLP,),
        in_specs=[
            pl.BlockSpec((TM_MLP, D_MODEL), tok),
            pl.BlockSpec((TM_MLP, D_ATTN), tok),
            pl.BlockSpec((TM_MLP, D_LRU), tok),
            pl.BlockSpec((D_MODEL, D_MODEL), const, pipeline_mode=resident),
            pl.BlockSpec((1, D_MODEL), const),
            pl.BlockSpec((D_MODEL, D_FF), const, pipeline_mode=resident),
            pl.BlockSpec((D_FF, D_MODEL), const, pipeline_mode=resident),
            pl.BlockSpec((1, D_MODEL), const),
        ],
        out_specs=pl.BlockSpec((TM_MLP, D_MODEL), tok),
        out_shape=jax.ShapeDtypeStruct((n_tok, D_MODEL), F32),
        compiler_params=pltpu.CompilerParams(vmem_limit_bytes=VMEM_LIMIT),
        name="mlp",
    )(x2d, attn2d, rec2d, w_out, g_mlp, w_up, w_down, g_final)


def _rope_tables(seq_len):
    rows = seq_len // GRID_W
    row_ids = jnp.repeat(jnp.arange(rows), GRID_W).astype(F32)
    col_ids = jnp.tile(jnp.arange(GRID_W), rows).astype(F32)
    inv_freq = ROPE_THETA ** (-jnp.arange(0, ROPE_HALF, 2, dtype=F32) / ROPE_HALF)
    ang_r = row_ids[:, None] * inv_freq
    ang_c = col_ids[:, None] * inv_freq
    cos = jnp.concatenate([jnp.cos(ang_r)] * 2 + [jnp.cos(ang_c)] * 2, axis=-1)
    sin = jnp.concatenate([-jnp.sin(ang_r), jnp.sin(ang_r), -jnp.sin(ang_c), jnp.sin(ang_c)], axis=-1)
    return jnp.tile(cos, (1, LANES // HEAD_DIM)), jnp.tile(sin, (1, LANES // HEAD_DIM))


def _block_diag_halves(w):
    per_half = N_LRU_BLOCKS // 2
    halves = [jax.scipy.linalg.block_diag(*[w[c * per_half + n] for n in range(per_half)])
              for c in range(2)]
    return jnp.stack(halves).astype(BF16)


def _layer(x, params, seq_tables):
    b, s, _ = x.shape
    x2d = x.reshape(b * s, D_MODEL)
    q, kd, va, xbr, ybr = _proj_call(x2d, params["g_mix"], params["w_in"], params["gqk"],
                                     params["seg"], seq_tables[0], seq_tables[1], s)
    attn = _attention(q.reshape(b, s, D_ATTN), kd.reshape(b, s, D_KDUP), va.reshape(b, s, D_KDUP),
                      params["score_bound"])
    rec = _lru_call(xbr.reshape(b, s, D_LRU), ybr.reshape(b, s, D_LRU), params["conv_w"],
                    params["conv_b"], params["wa"], params["ba"], params["wx"], params["bx"],
                    params["lam"])
    y = _mlp_call(x2d, attn.reshape(b * s, D_ATTN), rec.reshape(b * s, D_LRU), params["w_out"],
                  params["g_mlp"], params["w_up"], params["w_down"], params["g_final"])
    return y.reshape(b, s, D_MODEL)


def kernel(x_prompt, x_sample, norm_mix_g, w_in, q_norm_g, k_norm_g, conv_w, conv_b, lru_wa, lru_ba,
           lru_wx, lru_bx, lru_lambda, w_out, norm_mlp_g, w_up, w_down, norm_final_g):
    l = 0
    q_gain = q_norm_g[l] * (HEAD_DIM ** -0.5 * math.log2(math.e))
    gqk = jnp.concatenate([jnp.tile(q_gain, N_Q_HEADS), jnp.tile(k_norm_g[l], N_KV_HEADS)])
    seg = jax.scipy.linalg.block_diag(*[jnp.ones((HEAD_DIM, HEAD_DIM), F32)] * (MXU_DIM // HEAD_DIM))
    score_bound = (HEAD_DIM * SCORE_BOUND_SLACK) * jnp.max(jnp.abs(q_gain)) * jnp.max(jnp.abs(k_norm_g[l]))
    params = {
        "score_bound": score_bound,
        "g_mix": norm_mix_g[l][None, :],
        "w_in": w_in[l].astype(BF16),
        "gqk": gqk[None, :],
        "seg": seg.astype(BF16),
        "conv_w": conv_w[l],
        "conv_b": conv_b[l][None, :],
        "wa": jnp.stack([_block_diag_halves(0.5 * lru_wa[l, d]) for d in range(2)]),
        "ba": 0.5 * lru_ba[l],
        "wx": jnp.stack([_block_diag_halves(0.5 * lru_wx[l, d]) for d in range(2)]),
        "bx": 0.5 * lru_bx[l],
        "lam": lru_lambda[l],
        "w_out": w_out[l].astype(BF16),
        "g_mlp": norm_mlp_g[l][None, :],
        "w_up": w_up[l].astype(BF16),
        "w_down": w_down[l].astype(BF16),
        "g_final": norm_final_g[None, :],
    }
    outs = []
    for x in (x_prompt, x_sample):
        outs.append(_layer(x, params, _rope_tables(x.shape[1])))
    return tuple(outs)
```

```python
import functools
import math

import jax
import jax.numpy as jnp
from jax import lax
from jax.experimental import pallas as pl
from jax.experimental.pallas import tpu as pltpu

F32 = jnp.float32
BF16 = jnp.bfloat16

D_MODEL = 1024
HEAD_DIM = 64
N_Q_HEADS = 8
N_KV_HEADS = 2
Q_PER_KV = N_Q_HEADS // N_KV_HEADS
D_ATTN = N_Q_HEADS * HEAD_DIM
D_KV = N_KV_HEADS * HEAD_DIM
D_LRU = 512
N_LRU_BLOCKS = 8
LRU_BLOCK = D_LRU // N_LRU_BLOCKS
LRU_C = 8.0
CONV_W = 4
D_FF = 4 * D_MODEL
GRID_W = 64
ROPE_HALF = HEAD_DIM // 2
ROPE_QUARTER = ROPE_HALF // 2
ROPE_THETA = 10000.0
EPS = 1e-6

LANES = 128
SUBLANES = 8
MXU_DIM = 256

D_PROJ = D_ATTN + 2 * D_KV + 2 * D_LRU
D_QK = D_ATTN + D_KV
D_KDUP = 2 * D_KV

TM_PROJ = 1024
TQ = 512
TQ_ONE_STEP = 1024
TK_STEP = 4096
TK_CHUNK = 512
T_LRU = 1024
LRU_CHUNK = T_LRU // SUBLANES + 4
LRU_PAD_ROWS = SUBLANES * LRU_CHUNK
LRU_SLABS = D_LRU // LANES
LRU_UNROLL = 4
TM_MLP = 512
FF_CHUNK = 1024
VMEM_LIMIT = 56 * 1024 * 1024

NEG_BIG = -1e30
MAX_SHIFTED_RANGE = 100.0
SCORE_BOUND_SLACK = 1.0 + 2.0 ** -6


def _rms_scale(x):
    return lax.rsqrt(jnp.mean(x * x, axis=-1, keepdims=True) + EPS)


def _proj_kernel(x_ref, g_ref, w_ref, gqk_ref, seg_ref, cos_ref, sin_ref,
                 q_ref, k_ref, v_ref, xbr_ref, ybr_ref):
    x = x_ref[...]
    h = (x * _rms_scale(x) * g_ref[...]).astype(BF16)
    proj = jnp.dot(h, w_ref[...], preferred_element_type=F32)

    cos = cos_ref[...]
    sin = sin_ref[...]
    lane = lax.broadcasted_iota(jnp.int32, cos.shape, 1)
    first_half = (lane & (ROPE_HALF - 1)) < ROPE_QUARTER
    low_half = lane < HEAD_DIM

    def norm_rope(t, seg, gain):
        ssum = jnp.dot((t * t).astype(BF16), seg, preferred_element_type=F32)
        tn = t * lax.rsqrt(ssum * (1.0 / HEAD_DIM) + EPS) * gain
        out = []
        for s in range(t.shape[1] // LANES):
            u = tn[:, s * LANES:(s + 1) * LANES]
            partner = jnp.where(first_half,
                                pltpu.roll(u, LANES - ROPE_QUARTER, 1),
                                pltpu.roll(u, ROPE_QUARTER, 1))
            out.append(u * cos + partner * sin)
        return out

    for c in range(D_ATTN // MXU_DIM):
        cols = slice(c * MXU_DIM, (c + 1) * MXU_DIM)
        for s, r in enumerate(norm_rope(proj[:, cols], seg_ref[...], gqk_ref[:, cols])):
            q_ref[:, c * MXU_DIM + s * LANES:c * MXU_DIM + (s + 1) * LANES] = r.astype(BF16)

    (k01,) = norm_rope(proj[:, D_ATTN:D_ATTN + D_KV], seg_ref[0:LANES, 0:LANES],
                       gqk_ref[:, D_ATTN:D_ATTN + D_KV])
    unit_lane = (lane == HEAD_DIM).astype(F32)
    k_ref[:, 0:LANES] = jnp.where(low_half, k01, unit_lane).astype(BF16)
    k_ref[:, LANES:] = jnp.where(low_half, pltpu.roll(k01, HEAD_DIM, 1), unit_lane).astype(BF16)
    v01 = proj[:, D_ATTN + D_KV:D_ATTN + 2 * D_KV]
    v_ref[:, 0:LANES] = jnp.where(low_half, v01, 1.0).astype(BF16)
    v_ref[:, LANES:] = jnp.where(low_half, pltpu.roll(v01, HEAD_DIM, 1), 1.0).astype(BF16)
    xbr_ref[...] = proj[:, D_ATTN + 2 * D_KV:D_ATTN + 2 * D_KV + D_LRU].astype(BF16)
    ybr_ref[...] = proj[:, D_ATTN + 2 * D_KV + D_LRU:].astype(BF16)


def _proj_call(x2d, g_mix, w_in, gqk, seg, cos_t, sin_t, seq_len):
    n_tok = x2d.shape[0]
    tiles_per_seq = seq_len // TM_PROJ
    const = lambda i: (0, 0)
    tok = lambda i: (i, 0)
    pos = lambda i: (i % tiles_per_seq, 0)
    return pl.pallas_call(
        _proj_kernel,
        grid=(n_tok // TM_PROJ,),
        in_specs=[
            pl.BlockSpec((TM_PROJ, D_MODEL), tok),
            pl.BlockSpec((1, D_MODEL), const),
            pl.BlockSpec((D_MODEL, D_PROJ), const),
            pl.BlockSpec((1, D_QK), const),
            pl.BlockSpec((MXU_DIM, MXU_DIM), const),
            pl.BlockSpec((TM_PROJ, LANES), pos),
            pl.BlockSpec((TM_PROJ, LANES), pos),
        ],
        out_specs=[
            pl.BlockSpec((TM_PROJ, D_ATTN), tok),
            pl.BlockSpec((TM_PROJ, D_KDUP), tok),
            pl.BlockSpec((TM_PROJ, D_KDUP), tok),
            pl.BlockSpec((TM_PROJ, D_LRU), tok),
            pl.BlockSpec((TM_PROJ, D_LRU), tok),
        ],
        out_shape=[
            jax.ShapeDtypeStruct((n_tok, D_ATTN), BF16),
            jax.ShapeDtypeStruct((n_tok, D_KDUP), BF16),
            jax.ShapeDtypeStruct((n_tok, D_KDUP), BF16),
            jax.ShapeDtypeStruct((n_tok, D_LRU), BF16),
            jax.ShapeDtypeStruct((n_tok, D_LRU), BF16),
        ],
        compiler_params=pltpu.CompilerParams(vmem_limit_bytes=VMEM_LIMIT),
        name="proj",
    )(x2d, g_mix, w_in, gqk, seg, cos_t, sin_t)


def _attn_kernel(flag_ref, q_ref, k_ref, v_ref, shift_ref, o_ref, qs_ref, acc_ref, m_ref, *, n_kv):
    j = pl.program_id(3)
    bounded = flag_ref[0] != 0
    tq, tk_step = q_ref.shape[0], k_ref.shape[0]
    ts = min(tq, TQ)
    rows_s = Q_PER_KV * ts
    lane = lax.broadcasted_iota(jnp.int32, (ts, LANES), 1)
    low_half = lane < HEAD_DIM
    chunks = [slice(c * TK_CHUNK, (c + 1) * TK_CHUNK) for c in range(tk_step // TK_CHUNK)]

    def stack_queries(sub):
        for hh in range(Q_PER_KV):
            pair = q_ref[sub * ts:(sub + 1) * ts, (hh // 2) * LANES:(hh // 2 + 1) * LANES].astype(F32)
            if hh % 2:
                pair = pltpu.roll(pair, HEAD_DIM, 1)
            base = sub * rows_s + hh * ts
            qs_ref[base:base + ts, :] = jnp.where(low_half, pair, shift_ref[...]).astype(BF16)

    def scores(sub, keys):
        return lax.dot_general(qs_ref[sub * rows_s:(sub + 1) * rows_s, :], k_ref[keys, :],
                               (((1,), (1,)), ((), ())),
                               preferred_element_type=F32)

    def bounded_chunks(sub, first_assigns):
        acc_rows = slice(sub * rows_s, (sub + 1) * rows_s)
        for c, keys in enumerate(chunks):
            p = jnp.exp2(scores(sub, keys)).astype(BF16)
            part = jnp.dot(p, v_ref[keys, :], preferred_element_type=F32)
            if first_assigns and c == 0:
                acc_ref[acc_rows, :] = part
            else:
                acc_ref[acc_rows, :] += part

    def online_chunks(sub):
        acc_rows = slice(sub * rows_s, (sub + 1) * rows_s)
        m = m_ref[acc_rows, :]
        acc = acc_ref[acc_rows, :]
        for keys in chunks:
            s = scores(sub, keys)
            m_new = jnp.maximum(m, jnp.max(s, axis=1, keepdims=True))
            alpha = jnp.exp2(m - m_new)
            p = jnp.exp2(s - jnp.tile(m_new, (1, TK_CHUNK // LANES))).astype(BF16)
            acc = alpha * acc + jnp.dot(p, v_ref[keys, :], preferred_element_type=F32)
            m = m_new
        m_ref[acc_rows, :] = m
        acc_ref[acc_rows, :] = acc

    def init_online():
        acc_ref[...] = jnp.zeros(acc_ref.shape, F32)
        m_ref[...] = jnp.full(m_ref.shape, NEG_BIG, F32)

    def finalize(sub):
        a = acc_ref[sub * rows_s:(sub + 1) * rows_s, :]
        swapped = pltpu.roll(a, HEAD_DIM, 1)
        for pr in range(Q_PER_KV // 2):
            ev = slice((2 * pr) * ts, (2 * pr + 1) * ts)
            od = slice((2 * pr + 1) * ts, (2 * pr + 2) * ts)
            num = jnp.where(low_half, a[ev], swapped[od])
            den = jnp.where(low_half, swapped[ev], a[od])
            o_ref[sub * ts:(sub + 1) * ts, pr * LANES:(pr + 1) * LANES] = (num / den).astype(BF16)

    subs = range(tq // ts)
    if n_kv == 1:
        @pl.when(bounded)
        def _():
            for sub in subs:
                stack_queries(sub)
                bounded_chunks(sub, first_assigns=True)
                finalize(sub)

        @pl.when(jnp.logical_not(bounded))
        def _():
            init_online()
            for sub in subs:
                stack_queries(sub)
                online_chunks(sub)
                finalize(sub)
    else:
        @pl.when(j == 0)
        def _():
            for sub in subs:
                stack_queries(sub)
            init_online()

        @pl.when(bounded)
        def _():
            for sub in subs:
                bounded_chunks(sub, first_assigns=False)

        @pl.when(jnp.logical_not(bounded))
        def _():
            for sub in subs:
                online_chunks(sub)

        @pl.when(j == n_kv - 1)
        def _():
            for sub in subs:
                finalize(sub)


def _attention(q, ka, va, score_bound):
    b, s, _ = q.shape
    tk_step = min(s, TK_STEP)
    n_kv = s // tk_step
    tq = min(s, TQ_ONE_STEP if n_kv == 1 else TQ)
    n_q = s // tq
    rows = Q_PER_KV * tq
    use_bounded = 2.0 * score_bound <= MAX_SHIFTED_RANGE
    flag = use_bounded.astype(jnp.int32).reshape(1)
    shift = jnp.zeros((1, LANES), F32).at[0, HEAD_DIM].set(jnp.where(use_bounded, -score_bound, 0.0))
    return pl.pallas_call(
        functools.partial(_attn_kernel, n_kv=n_kv),
        grid=(b, N_KV_HEADS, n_q, n_kv),
        in_specs=[
            pl.BlockSpec(memory_space=pltpu.SMEM),
            pl.BlockSpec((None, tq, Q_PER_KV * HEAD_DIM), lambda bi, g, i, j: (bi, i, g)),
            pl.BlockSpec((None, tk_step, LANES), lambda bi, g, i, j: (bi, j, g)),
            pl.BlockSpec((None, tk_step, LANES), lambda bi, g, i, j: (bi, j, g)),
            pl.BlockSpec((1, LANES), lambda bi, g, i, j: (0, 0)),
        ],
        out_specs=pl.BlockSpec((None, tq, Q_PER_KV * HEAD_DIM), lambda bi, g, i, j: (bi, i, g)),
        out_shape=jax.ShapeDtypeStruct((b, s, D_ATTN), BF16),
        scratch_shapes=[
            pltpu.VMEM((rows, LANES), BF16),
            pltpu.VMEM((rows, LANES), F32),
            pltpu.VMEM((rows, LANES), F32),
        ],
        compiler_params=pltpu.CompilerParams(vmem_limit_bytes=VMEM_LIMIT),
        name="attn",
    )(flag, q, ka, va, shift)


def _lru_conv(x_ref, prev_ref, next_ref, cw_ref, cb_ref, has_prev, has_next):
    t_rows = x_ref.shape[0]
    x = x_ref[...].astype(F32)
    prev = prev_ref[...].astype(F32) * has_prev
    nxt = next_ref[...].astype(F32) * has_next
    row8 = lax.broadcasted_iota(jnp.int32, (SUBLANES, D_LRU), 0)

    def shifted(k):
        if k == 0:
            return x
        rolled = pltpu.roll(x, (-k) % t_rows, 0)
        if k < 0:
            head = jnp.where(row8 < -k, pltpu.roll(prev, -k, 0), rolled[0:SUBLANES])
            return jnp.concatenate([head, rolled[SUBLANES:]], axis=0)
        tail = jnp.where(row8 >= SUBLANES - k, pltpu.roll(nxt, SUBLANES - k, 0), rolled[t_rows - SUBLANES:])
        return jnp.concatenate([rolled[:t_rows - SUBLANES], tail], axis=0)

    xc = cb_ref[...]
    for w in range(CONV_W):
        xc = xc + cw_ref[w:w + 1, :] * shifted(w - 2)
    return xc


def _lru_coeffs(xc, wa_ref, ba_ref, wx_ref, bx_ref, lam_ref, a_ref, u_ref):
    t_rows = xc.shape[0]
    xcb = xc.astype(BF16)
    half = D_LRU // 2
    r_parts, i_parts = [], []
    for c in range(2):
        xh = xcb[:, c * half:(c + 1) * half]
        r_parts.append(jnp.dot(xh, wa_ref[c], preferred_element_type=F32))
        i_parts.append(jnp.dot(xh, wx_ref[c], preferred_element_type=F32))
    t_r = jnp.tanh(jnp.concatenate(r_parts, axis=1) + ba_ref[...])
    t_i = jnp.tanh(jnp.concatenate(i_parts, axis=1) + bx_ref[...])
    lam = lam_ref[...]
    neg_softplus = -(jnp.maximum(-lam, 0.0) + jnp.log1p(jnp.exp(-jnp.abs(lam))))
    log_a = (0.5 * LRU_C * neg_softplus) * (1.0 + t_r)
    a = jnp.exp(log_a)
    th = jnp.tanh(log_a)
    u = jnp.exp2(0.5 * jnp.log2(-0.5 * th / (1.0 - th))) * ((1.0 + t_i) * xc)
    for lt in range(LRU_SLABS):
        a_ref[lt, 0:t_rows, :] = a[:, lt * LANES:(lt + 1) * LANES]
        u_ref[lt, 0:t_rows, :] = u[:, lt * LANES:(lt + 1) * LANES]
        a_ref[lt, t_rows:, :] = jnp.ones((LRU_PAD_ROWS - t_rows, LANES), F32)
        u_ref[lt, t_rows:, :] = jnp.zeros((LRU_PAD_ROWS - t_rows, LANES), F32)


def _sublane_shift(x, k, fill, row, reverse):
    if reverse:
        return jnp.where(row < SUBLANES - k, pltpu.roll(x, SUBLANES - k, 0), fill)
    return jnp.where(row >= k, pltpu.roll(x, k, 0), fill)


def _scan_tile(a_ref, u_ref, h_ref, carry_ref, reverse):
    row = lax.broadcasted_iota(jnp.int32, (SUBLANES, LANES), 0)

    def step_rows(t):
        tt = (LRU_CHUNK - 1 - t) if reverse else t
        return pl.ds(tt, SUBLANES, stride=LRU_CHUNK)

    def reduce_step(t, au):
        rows = step_rows(t)
        out_a, out_u = [], []
        for lt in range(LRU_SLABS):
            a_t = a_ref[lt, rows, :]
            out_u.append(a_t * au[1][lt] + u_ref[lt, rows, :])
            out_a.append(a_t * au[0][lt])
        return tuple(out_a), tuple(out_u)

    ones = tuple(jnp.ones((SUBLANES, LANES), F32) for _ in range(LRU_SLABS))
    zeros = tuple(jnp.zeros((SUBLANES, LANES), F32) for _ in range(LRU_SLABS))
    tot_a, tot_u = lax.fori_loop(0, LRU_CHUNK, reduce_step, (ones, zeros), unroll=LRU_UNROLL)

    h_in = []
    for lt in range(LRU_SLABS):
        a, u = tot_a[lt], tot_u[lt]
        for k in (1, 2, 4):
            u = u + a * _sublane_shift(u, k, 0.0, row, reverse)
            a = a * _sublane_shift(a, k, 1.0, row, reverse)
        carry = carry_ref[lt]
        h_out = u + a * carry
        h_in.append(_sublane_shift(h_out, 1, carry, row, reverse))
        last = h_out[0:1, :] if reverse else h_out[SUBLANES - 1:SUBLANES, :]
        carry_ref[lt] = jnp.broadcast_to(last, (SUBLANES, LANES))

    def scan_step(t, hs):
        rows = step_rows(t)
        out = []
        for lt in range(LRU_SLABS):
            h = a_ref[lt, rows, :] * hs[lt] + u_ref[lt, rows, :]
            h_ref[lt, rows, :] = h
            out.append(h)
        return tuple(out)

    lax.fori_loop(0, LRU_CHUNK, scan_step, tuple(h_in), unroll=LRU_UNROLL)


def _lru_fwd_kernel(x_ref, prev_ref, next_ref, cw_ref, cb_ref, wa_ref, ba_ref, wx_ref, bx_ref, lam_ref,
                    hf_ref, xc_ref, a_ref, u_ref, h_ref, carry_ref, *, n_tiles):
    i = pl.program_id(1)

    @pl.when(i == 0)
    def _():
        carry_ref[...] = jnp.zeros(carry_ref.shape, F32)

    has_prev = (i > 0).astype(F32)
    has_next = (i < n_tiles - 1).astype(F32)
    xc = _lru_conv(x_ref, prev_ref, next_ref, cw_ref, cb_ref, has_prev, has_next)
    xc_ref[...] = xc
    _lru_coeffs(xc, wa_ref, ba_ref, wx_ref, bx_ref, lam_ref, a_ref, u_ref)
    _scan_tile(a_ref, u_ref, h_ref, carry_ref, reverse=False)
    for lt in range(LRU_SLABS):
        hf_ref[:, lt * LANES:(lt + 1) * LANES] = h_ref[lt, 0:T_LRU, :]


def _lru_bwd_kernel(xc_ref, y_ref, hf_ref, wa_ref, ba_ref, wx_ref, bx_ref, lam_ref,
                    o_ref, a_ref, u_ref, h_ref, carry_ref):
    @pl.when(pl.program_id(1) == 0)
    def _():
        carry_ref[...] = jnp.zeros(carry_ref.shape, F32)

    _lru_coeffs(xc_ref[...], wa_ref, ba_ref, wx_ref, bx_ref, lam_ref, a_ref, u_ref)
    _scan_tile(a_ref, u_ref, h_ref, carry_ref, reverse=True)
    for lt in range(LRU_SLABS):
        cols = slice(lt * LANES, (lt + 1) * LANES)
        y = y_ref[:, cols].astype(F32)
        gelu = 0.5 * y * (1.0 + jnp.tanh(math.sqrt(2.0 / math.pi) * (y + 0.044715 * (y * y * y))))
        o_ref[:, cols] = ((hf_ref[:, cols] + h_ref[lt, 0:T_LRU, :]) * gelu).astype(BF16)


def _lru_call(xbr, ybr, conv_w, conv_b, wa, ba, wx, bx, lam):
    b, s, _ = xbr.shape
    n_tiles = s // T_LRU
    blocks_per_tile = T_LRU // SUBLANES
    n_blocks = n_tiles * blocks_per_tile
    half = D_LRU // 2
    slab = pltpu.VMEM((LRU_SLABS, LRU_PAD_ROWS, LANES), F32)
    scratch = [slab, slab, slab, pltpu.VMEM((LRU_SLABS, SUBLANES, LANES), F32)]
    params = pltpu.CompilerParams(vmem_limit_bytes=VMEM_LIMIT)
    c2 = lambda bi, i: (0, 0)
    c3 = lambda bi, i: (0, 0, 0)
    row_spec = pl.BlockSpec((1, D_LRU), c2)
    gate_specs = [pl.BlockSpec((2, half, half), c3), row_spec, pl.BlockSpec((2, half, half), c3),
                  row_spec, row_spec]

    fwd_tile = pl.BlockSpec((None, T_LRU, D_LRU), lambda bi, i: (bi, i, 0))
    prev = lambda bi, i: (bi, jnp.maximum(i * blocks_per_tile - 1, 0), 0)
    nxt = lambda bi, i: (bi, jnp.minimum((i + 1) * blocks_per_tile, n_blocks - 1), 0)
    h_fwd, xc = pl.pallas_call(
        functools.partial(_lru_fwd_kernel, n_tiles=n_tiles),
        grid=(b, n_tiles),
        in_specs=[fwd_tile, pl.BlockSpec((None, SUBLANES, D_LRU), prev),
                  pl.BlockSpec((None, SUBLANES, D_LRU), nxt),
                  pl.BlockSpec((CONV_W, D_LRU), c2), row_spec] + gate_specs,
        out_specs=[fwd_tile, fwd_tile],
        out_shape=[jax.ShapeDtypeStruct((b, s, D_LRU), F32), jax.ShapeDtypeStruct((b, s, D_LRU), F32)],
        scratch_shapes=scratch,
        compiler_params=params,
        name="lru_fwd",
    )(xbr, xbr, xbr, conv_w, conv_b, wa[0], ba[0:1], wx[0], bx[0:1], lam[0:1])

    bwd_tile = pl.BlockSpec((None, T_LRU, D_LRU), lambda bi, i: (bi, n_tiles - 1 - i, 0))
    return pl.pallas_call(
        _lru_bwd_kernel,
        grid=(b, n_tiles),
        in_specs=[bwd_tile, bwd_tile, bwd_tile] + gate_specs,
        out_specs=bwd_tile,
        out_shape=jax.ShapeDtypeStruct((b, s, D_LRU), BF16),
        scratch_shapes=scratch,
        compiler_params=params,
        name="lru_bwd",
    )(xc, ybr, h_fwd, wa[1], ba[1:2], wx[1], bx[1:2], lam[1:2])


def _mlp_kernel(x_ref, attn_ref, rec_ref, wo_ref, g2_ref, wup_ref, wdn_ref, gf_ref, o_ref):
    x1 = (x_ref[...]
          + jnp.dot(attn_ref[...], wo_ref[0:D_ATTN, :], preferred_element_type=F32)
          + jnp.dot(rec_ref[...], wo_ref[D_ATTN:, :], preferred_element_type=F32))
    h = (x1 * _rms_scale(x1) * g2_ref[...]).astype(BF16)
    mlp = None
    for c in range(D_FF // FF_CHUNK):
        up = jnp.dot(h, wup_ref[:, c * FF_CHUNK:(c + 1) * FF_CHUNK], preferred_element_type=F32)
        act = jnp.square(jnp.maximum(up, 0.0)).astype(BF16)
        down = jnp.dot(act, wdn_ref[c * FF_CHUNK:(c + 1) * FF_CHUNK, :], preferred_element_type=F32)
        mlp = down if mlp is None else mlp + down
    x2 = x1 + mlp
    o_ref[...] = x2 * _rms_scale(x2) * gf_ref[...]


def _mlp_call(x2d, attn2d, rec2d, w_out, g_mlp, w_up, w_down, g_final):
    n_tok = x2d.shape[0]
    const = lambda i: (0, 0)
    tok = lambda i: (i, 0)
    resident = pl.Buffered(1)
    return pl.pallas_call(
        _mlp_kernel,
        grid=(n_tok // TM_MLP,),
        in_specs=[
            pl.BlockSpec((TM_MLP, D_MODEL), tok),
            pl.BlockSpec((TM_MLP, D_ATTN), tok),
            pl.BlockSpec((TM_MLP, D_LRU), tok),
            pl.BlockSpec((D_MODEL, D_MODEL), const, pipeline_mode=resident),
            pl.BlockSpec((1, D_MODEL), const),
            pl.BlockSpec((D_MODEL, D_FF), const, pipeline_mode=resident),
            pl.BlockSpec((D_FF, D_MODEL), const, pipeline_mode=resident),
            pl.BlockSpec((1, D_MODEL), const),
        ],
        out_specs=pl.BlockSpec((TM_MLP, D_MODEL), tok),
        out_shape=jax.ShapeDtypeStruct((n_tok, D_MODEL), F32),
        compiler_params=pltpu.CompilerParams(vmem_limit_bytes=VMEM_LIMIT),
        name="mlp",
    )(x2d, attn2d, rec2d, w_out, g_mlp, w_up, w_down, g_final)


def _rope_tables(seq_len):
    rows = seq_len // GRID_W
    row_ids = jnp.repeat(jnp.arange(rows), GRID_W).astype(F32)
    col_ids = jnp.tile(jnp.arange(GRID_W), rows).astype(F32)
    inv_freq = ROPE_THETA ** (-jnp.arange(0, ROPE_HALF, 2, dtype=F32) / ROPE_HALF)
    ang_r = row_ids[:, None] * inv_freq
    ang_c = col_ids[:, None] * inv_freq
    cos = jnp.concatenate([jnp.cos(ang_r)] * 2 + [jnp.cos(ang_c)] * 2, axis=-1)
    sin = jnp.concatenate([-jnp.sin(ang_r), jnp.sin(ang_r), -jnp.sin(ang_c), jnp.sin(ang_c)], axis=-1)
    return jnp.tile(cos, (1, LANES // HEAD_DIM)), jnp.tile(sin, (1, LANES // HEAD_DIM))


def _block_diag_halves(w):
    per_half = N_LRU_BLOCKS // 2
    halves = [jax.scipy.linalg.block_diag(*[w[c * per_half + n] for n in range(per_half)])
              for c in range(2)]
    return jnp.stack(halves).astype(BF16)


def _layer(x, params, seq_tables):
    b, s, _ = x.shape
    x2d = x.reshape(b * s, D_MODEL)
    q, kd, va, xbr, ybr = _proj_call(x2d, params["g_mix"], params["w_in"], params["gqk"],
                                     params["seg"], seq_tables[0], seq_tables[1], s)
    attn = _attention(q.reshape(b, s, D_ATTN), kd.reshape(b, s, D_KDUP), va.reshape(b, s, D_KDUP),
                      params["score_bound"])
    rec = _lru_call(xbr.reshape(b, s, D_LRU), ybr.reshape(b, s, D_LRU), params["conv_w"],
                    params["conv_b"], params["wa"], params["ba"], params["wx"], params["bx"],
                    params["lam"])
    y = _mlp_call(x2d, attn.reshape(b * s, D_ATTN), rec.reshape(b * s, D_LRU), params["w_out"],
                  params["g_mlp"], params["w_up"], params["w_down"], params["g_final"])
    return y.reshape(b, s, D_MODEL)


def kernel(x_prompt, x_sample, norm_mix_g, w_in, q_norm_g, k_norm_g, conv_w, conv_b, lru_wa, lru_ba,
           lru_wx, lru_bx, lru_lambda, w_out, norm_mlp_g, w_up, w_down, norm_final_g):
    l = 0
    q_gain = q_norm_g[l] * (HEAD_DIM ** -0.5 * math.log2(math.e))
    gqk = jnp.concatenate([jnp.tile(q_gain, N_Q_HEADS), jnp.tile(k_norm_g[l], N_KV_HEADS)])
    seg = jax.scipy.linalg.block_diag(*[jnp.ones((HEAD_DIM, HEAD_DIM), F32)] * (MXU_DIM // HEAD_DIM))
    score_bound = (HEAD_DIM * SCORE_BOUND_SLACK) * jnp.max(jnp.abs(q_gain)) * jnp.max(jnp.abs(k_norm_g[l]))
    params = {
        "score_bound": score_bound,
        "g_mix": norm_mix_g[l][None, :],
        "w_in": w_in[l].astype(BF16),
        "gqk": gqk[None, :],
        "seg": seg.astype(BF16),
        "conv_w": conv_w[l],
        "conv_b": conv_b[l][None, :],
        "wa": jnp.stack([_block_diag_halves(0.5 * lru_wa[l, d]) for d in range(2)]),
        "ba": 0.5 * lru_ba[l],
        "wx": jnp.stack([_block_diag_halves(0.5 * lru_wx[l, d]) for d in range(2)]),
        "bx": 0.5 * lru_bx[l],
        "lam": lru_lambda[l],
        "w_out": w_out[l].astype(BF16),
        "g_mlp": norm_mlp_g[l][None, :],
        "w_up": w_up[l].astype(BF16),
        "w_down": w_down[l].astype(BF16),
        "g_final": norm_final_g[None, :],
    }
    outs = []
    for x in (x_prompt, x_sample):
        outs.append(_layer(x, params, _rope_tables(x.shape[1])))
    return tuple(outs)
```

```python
import functools
import math

import jax
import jax.numpy as jnp
from jax import lax
from jax.experimental import pallas as pl
from jax.experimental.pallas import tpu as pltpu

F32 = jnp.float32
BF16 = jnp.bfloat16

D_MODEL = 1024
HEAD_DIM = 64
N_Q_HEADS = 8
N_KV_HEADS = 2
Q_PER_KV = N_Q_HEADS // N_KV_HEADS
D_ATTN = N_Q_HEADS * HEAD_DIM
D_KV = N_KV_HEADS * HEAD_DIM
D_LRU = 512
N_LRU_BLOCKS = 8
LRU_BLOCK = D_LRU // N_LRU_BLOCKS
LRU_C = 8.0
CONV_W = 4
D_FF = 4 * D_MODEL
GRID_W = 64
ROPE_HALF = HEAD_DIM // 2
ROPE_QUARTER = ROPE_HALF // 2
ROPE_THETA = 10000.0
EPS = 1e-6
GELU_C = math.sqrt(2.0 / math.pi)

LANES = 128
SUBLANES = 8
MXU_DIM = 256

D_PROJ = D_ATTN + 2 * D_KV + 2 * D_LRU
D_QK = D_ATTN + D_KV
D_KDUP = 2 * D_KV

TM_PROJ = 1024
TQ = 512
TQ_ONE_STEP = 1024
TK_STEP = 4096
TK_CHUNK = 512
T_LRU = 2048
LRU_CHUNK = T_LRU // SUBLANES + 4
LRU_PAD_ROWS = SUBLANES * LRU_CHUNK
LRU_SLABS = D_LRU // LANES
LRU_UNROLL = 4
TM_MLP = 512
FF_CHUNK = 1024
VMEM_LIMIT = 56 * 1024 * 1024

NEG_BIG = -1e30
MAX_SHIFTED_RANGE = 100.0
SCORE_BOUND_SLACK = 1.0 + 2.0 ** -6


def _rms_scale(x):
    return lax.rsqrt(jnp.mean(x * x, axis=-1, keepdims=True) + EPS)


def _proj_kernel(x_ref, g_ref, w_ref, gqk_ref, seg_ref, cos_ref, sin_ref,
                 q_ref, k_ref, v_ref, xbr_ref, ybr_ref):
    x = x_ref[...]
    proj = jnp.dot((x * g_ref[...]).astype(BF16), w_ref[...], preferred_element_type=F32)
    proj = proj * _rms_scale(x)

    cos = cos_ref[...]
    sin = sin_ref[...]
    lane = lax.broadcasted_iota(jnp.int32, cos.shape, 1)
    first_half = (lane & (ROPE_HALF - 1)) < ROPE_QUARTER
    low_half = lane < HEAD_DIM

    def norm_rope(t, seg, gain):
        ssum = jnp.dot((t * t).astype(BF16), seg, preferred_element_type=F32)
        tn = t * lax.rsqrt(ssum * (1.0 / HEAD_DIM) + EPS) * gain
        out = []
        for s in range(t.shape[1] // LANES):
            u = tn[:, s * LANES:(s + 1) * LANES]
            partner = jnp.where(first_half,
                                pltpu.roll(u, LANES - ROPE_QUARTER, 1),
                                pltpu.roll(u, ROPE_QUARTER, 1))
            out.append(u * cos + partner * sin)
        return out

    for c in range(D_ATTN // MXU_DIM):
        cols = slice(c * MXU_DIM, (c + 1) * MXU_DIM)
        for s, r in enumerate(norm_rope(proj[:, cols], seg_ref[...], gqk_ref[:, cols])):
            q_ref[:, c * MXU_DIM + s * LANES:c * MXU_DIM + (s + 1) * LANES] = r.astype(BF16)

    (k01,) = norm_rope(proj[:, D_ATTN:D_ATTN + D_KV], seg_ref[0:LANES, 0:LANES],
                       gqk_ref[:, D_ATTN:D_ATTN + D_KV])
    unit_lane = (lane == HEAD_DIM).astype(F32)
    k_ref[:, 0:LANES] = jnp.where(low_half, k01, unit_lane).astype(BF16)
    k_ref[:, LANES:] = jnp.where(low_half, pltpu.roll(k01, HEAD_DIM, 1), unit_lane).astype(BF16)
    v01 = proj[:, D_ATTN + D_KV:D_ATTN + 2 * D_KV]
    v_ref[:, 0:LANES] = jnp.where(low_half, v01, 1.0).astype(BF16)
    v_ref[:, LANES:] = jnp.where(low_half, pltpu.roll(v01, HEAD_DIM, 1), 1.0).astype(BF16)
    xbr_ref[...] = proj[:, D_ATTN + 2 * D_KV:D_ATTN + 2 * D_KV + D_LRU].astype(BF16)
    ybr_ref[...] = proj[:, D_ATTN + 2 * D_KV + D_LRU:].astype(BF16)


def _proj_call(x2d, g_mix, w_in, gqk, seg, cos_t, sin_t, seq_len):
    n_tok = x2d.shape[0]
    tiles_per_seq = seq_len // TM_PROJ
    const = lambda i: (0, 0)
    tok = lambda i: (i, 0)
    pos = lambda i: (i % tiles_per_seq, 0)
    return pl.pallas_call(
        _proj_kernel,
        grid=(n_tok // TM_PROJ,),
        in_specs=[
            pl.BlockSpec((TM_PROJ, D_MODEL), tok),
            pl.BlockSpec((1, D_MODEL), const),
            pl.BlockSpec((D_MODEL, D_PROJ), const),
            pl.BlockSpec((1, D_QK), const),
            pl.BlockSpec((MXU_DIM, MXU_DIM), const),
            pl.BlockSpec((TM_PROJ, LANES), pos),
            pl.BlockSpec((TM_PROJ, LANES), pos),
        ],
        out_specs=[
            pl.BlockSpec((TM_PROJ, D_ATTN), tok),
            pl.BlockSpec((TM_PROJ, D_KDUP), tok),
            pl.BlockSpec((TM_PROJ, D_KDUP), tok),
            pl.BlockSpec((TM_PROJ, D_LRU), tok),
            pl.BlockSpec((TM_PROJ, D_LRU), tok),
        ],
        out_shape=[
            jax.ShapeDtypeStruct((n_tok, D_ATTN), BF16),
            jax.ShapeDtypeStruct((n_tok, D_KDUP), BF16),
            jax.ShapeDtypeStruct((n_tok, D_KDUP), BF16),
            jax.ShapeDtypeStruct((n_tok, D_LRU), BF16),
            jax.ShapeDtypeStruct((n_tok, D_LRU), BF16),
        ],
        compiler_params=pltpu.CompilerParams(vmem_limit_bytes=VMEM_LIMIT),
        name="proj",
    )(x2d, g_mix, w_in, gqk, seg, cos_t, sin_t)


def _attn_kernel(flag_ref, q_ref, k_ref, v_ref, shift_ref, o_ref, qs_ref, acc_ref, m_ref, *, n_kv):
    j = pl.program_id(3)
    bounded = flag_ref[0] != 0
    tq, tk_step = q_ref.shape[0], k_ref.shape[0]
    ts = min(tq, TQ)
    rows_s = Q_PER_KV * ts
    lane = lax.broadcasted_iota(jnp.int32, (ts, LANES), 1)
    low_half = lane < HEAD_DIM
    chunks = [slice(c * TK_CHUNK, (c + 1) * TK_CHUNK) for c in range(tk_step // TK_CHUNK)]

    def stack_queries(sub):
        for hh in range(Q_PER_KV):
            pair = q_ref[sub * ts:(sub + 1) * ts, (hh // 2) * LANES:(hh // 2 + 1) * LANES].astype(F32)
            if hh % 2:
                pair = pltpu.roll(pair, HEAD_DIM, 1)
            base = sub * rows_s + hh * ts
            qs_ref[base:base + ts, :] = jnp.where(low_half, pair, shift_ref[...]).astype(BF16)

    def scores(sub, keys):
        return lax.dot_general(qs_ref[sub * rows_s:(sub + 1) * rows_s, :], k_ref[keys, :],
                               (((1,), (1,)), ((), ())),
                               preferred_element_type=F32)

    def bounded_chunks(sub, first_assigns):
        acc_rows = slice(sub * rows_s, (sub + 1) * rows_s)
        for c, keys in enumerate(chunks):
            p = jnp.exp2(scores(sub, keys)).astype(BF16)
            part = jnp.dot(p, v_ref[keys, :], preferred_element_type=F32)
            if first_assigns and c == 0:
                acc_ref[acc_rows, :] = part
            else:
                acc_ref[acc_rows, :] += part

    def online_chunks(sub):
        acc_rows = slice(sub * rows_s, (sub + 1) * rows_s)
        m = m_ref[acc_rows, :]
        acc = acc_ref[acc_rows, :]
        for keys in chunks:
            s = scores(sub, keys)
            m_new = jnp.maximum(m, jnp.max(s, axis=1, keepdims=True))
            alpha = jnp.exp2(m - m_new)
            p = jnp.exp2(s - jnp.tile(m_new, (1, TK_CHUNK // LANES))).astype(BF16)
            acc = alpha * acc + jnp.dot(p, v_ref[keys, :], preferred_element_type=F32)
            m = m_new
        m_ref[acc_rows, :] = m
        acc_ref[acc_rows, :] = acc

    def init_online():
        acc_ref[...] = jnp.zeros(acc_ref.shape, F32)
        m_ref[...] = jnp.full(m_ref.shape, NEG_BIG, F32)

    def finalize(sub):
        a = acc_ref[sub * rows_s:(sub + 1) * rows_s, :]
        swapped = pltpu.roll(a, HEAD_DIM, 1)
        for pr in range(Q_PER_KV // 2):
            ev = slice((2 * pr) * ts, (2 * pr + 1) * ts)
            od = slice((2 * pr + 1) * ts, (2 * pr + 2) * ts)
            num = jnp.where(low_half, a[ev], swapped[od])
            den = jnp.where(low_half, swapped[ev], a[od])
            o_ref[sub * ts:(sub + 1) * ts, pr * LANES:(pr + 1) * LANES] = (num / den).astype(BF16)

    subs = range(tq // ts)
    if n_kv == 1:
        @pl.when(bounded)
        def _():
            for sub in subs:
                stack_queries(sub)
                bounded_chunks(sub, first_assigns=True)
                finalize(sub)

        @pl.when(jnp.logical_not(bounded))
        def _():
            init_online()
            for sub in subs:
                stack_queries(sub)
                online_chunks(sub)
                finalize(sub)
    else:
        @pl.when(j == 0)
        def _():
            for sub in subs:
                stack_queries(sub)
            init_online()

        @pl.when(bounded)
        def _():
            for sub in subs:
                bounded_chunks(sub, first_assigns=False)

        @pl.when(jnp.logical_not(bounded))
        def _():
            for sub in subs:
                online_chunks(sub)

        @pl.when(j == n_kv - 1)
        def _():
            for sub in subs:
                finalize(sub)


def _attention(q, ka, va, score_bound):
    b, s, _ = q.shape
    tk_step = min(s, TK_STEP)
    n_kv = s // tk_step
    tq = min(s, TQ_ONE_STEP if n_kv == 1 else TQ)
    n_q = s // tq
    rows = Q_PER_KV * tq
    use_bounded = 2.0 * score_bound <= MAX_SHIFTED_RANGE
    flag = use_bounded.astype(jnp.int32).reshape(1)
    shift = jnp.zeros((1, LANES), F32).at[0, HEAD_DIM].set(jnp.where(use_bounded, -score_bound, 0.0))
    return pl.pallas_call(
        functools.partial(_attn_kernel, n_kv=n_kv),
        grid=(b, N_KV_HEADS, n_q, n_kv),
        in_specs=[
            pl.BlockSpec(memory_space=pltpu.SMEM),
            pl.BlockSpec((None, tq, Q_PER_KV * HEAD_DIM), lambda bi, g, i, j: (bi, i, g)),
            pl.BlockSpec((None, tk_step, LANES), lambda bi, g, i, j: (bi, j, g)),
            pl.BlockSpec((None, tk_step, LANES), lambda bi, g, i, j: (bi, j, g)),
            pl.BlockSpec((1, LANES), lambda bi, g, i, j: (0, 0)),
        ],
        out_specs=pl.BlockSpec((None, tq, Q_PER_KV * HEAD_DIM), lambda bi, g, i, j: (bi, i, g)),
        out_shape=jax.ShapeDtypeStruct((b, s, D_ATTN), BF16),
        scratch_shapes=[
            pltpu.VMEM((rows, LANES), BF16),
            pltpu.VMEM((rows, LANES), F32),
            pltpu.VMEM((rows, LANES), F32),
        ],
        compiler_params=pltpu.CompilerParams(vmem_limit_bytes=VMEM_LIMIT),
        name="attn",
    )(flag, q, ka, va, shift)


def _lru_conv(x_ref, prev_ref, next_ref, cw_ref, cb_ref, has_prev, has_next):
    t_rows = x_ref.shape[0]
    x = x_ref[...].astype(F32)
    prev = prev_ref[...].astype(F32) * has_prev
    nxt = next_ref[...].astype(F32) * has_next
    row8 = lax.broadcasted_iota(jnp.int32, (SUBLANES, D_LRU), 0)

    def shifted(k):
        if k == 0:
            return x
        rolled = pltpu.roll(x, (-k) % t_rows, 0)
        if k < 0:
            head = jnp.where(row8 < -k, pltpu.roll(prev, -k, 0), rolled[0:SUBLANES])
            return jnp.concatenate([head, rolled[SUBLANES:]], axis=0)
        tail = jnp.where(row8 >= SUBLANES - k, pltpu.roll(nxt, SUBLANES - k, 0), rolled[t_rows - SUBLANES:])
        return jnp.concatenate([rolled[:t_rows - SUBLANES], tail], axis=0)

    xc = cb_ref[...]
    for w in range(CONV_W):
        xc = xc + cw_ref[w:w + 1, :] * shifted(w - 2)
    return xc


def _lru_coeffs(xc, wa_ref, ba_ref, wx_ref, bx_ref, lam_ref, a_ref, u_ref):
    t_rows = xc.shape[0]
    xcb = xc.astype(BF16)
    half = D_LRU // 2
    r_parts, i_parts = [], []
    for c in range(2):
        xh = xcb[:, c * half:(c + 1) * half]
        r_parts.append(jnp.dot(xh, wa_ref[c], preferred_element_type=F32))
        i_parts.append(jnp.dot(xh, wx_ref[c], preferred_element_type=F32))
    t_r = jnp.tanh(jnp.concatenate(r_parts, axis=1) + ba_ref[...])
    t_i = jnp.tanh(jnp.concatenate(i_parts, axis=1) + bx_ref[...])
    lam = lam_ref[...]
    neg_softplus = -(jnp.maximum(-lam, 0.0) + jnp.log1p(jnp.exp(-jnp.abs(lam))))
    log_a = (0.5 * LRU_C * neg_softplus) * (1.0 + t_r)
    a = jnp.exp(log_a)
    th = jnp.tanh(log_a)
    u = jnp.exp2(0.5 * jnp.log2(-0.5 * th / (1.0 - th))) * ((1.0 + t_i) * xc)
    for lt in range(LRU_SLABS):
        a_ref[lt, 0:t_rows, :] = a[:, lt * LANES:(lt + 1) * LANES]
        u_ref[lt, 0:t_rows, :] = u[:, lt * LANES:(lt + 1) * LANES]
        a_ref[lt, t_rows:, :] = jnp.ones((LRU_PAD_ROWS - t_rows, LANES), F32)
        u_ref[lt, t_rows:, :] = jnp.zeros((LRU_PAD_ROWS - t_rows, LANES), F32)


def _sublane_shift(x, k, fill, row, reverse):
    if reverse:
        return jnp.where(row < SUBLANES - k, pltpu.roll(x, SUBLANES - k, 0), fill)
    return jnp.where(row >= k, pltpu.roll(x, k, 0), fill)


def _scan_tile(a_ref, u_ref, h_ref, carry_ref, reverse):
    row = lax.broadcasted_iota(jnp.int32, (SUBLANES, LANES), 0)

    def step_rows(t):
        tt = (LRU_CHUNK - 1 - t) if reverse else t
        return pl.ds(tt, SUBLANES, stride=LRU_CHUNK)

    def reduce_step(t, au):
        rows = step_rows(t)
        out_a, out_u = [], []
        for lt in range(LRU_SLABS):
            a_t = a_ref[lt, rows, :]
            out_u.append(a_t * au[1][lt] + u_ref[lt, rows, :])
            out_a.append(a_t * au[0][lt])
        return tuple(out_a), tuple(out_u)

    ones = tuple(jnp.ones((SUBLANES, LANES), F32) for _ in range(LRU_SLABS))
    zeros = tuple(jnp.zeros((SUBLANES, LANES), F32) for _ in range(LRU_SLABS))
    tot_a, tot_u = lax.fori_loop(0, LRU_CHUNK, reduce_step, (ones, zeros), unroll=LRU_UNROLL)

    h_in = []
    for lt in range(LRU_SLABS):
        a, u = tot_a[lt], tot_u[lt]
        for k in (1, 2, 4):
            u = u + a * _sublane_shift(u, k, 0.0, row, reverse)
            a = a * _sublane_shift(a, k, 1.0, row, reverse)
        carry = carry_ref[lt]
        h_out = u + a * carry
        h_in.append(_sublane_shift(h_out, 1, carry, row, reverse))
        last = h_out[0:1, :] if reverse else h_out[SUBLANES - 1:SUBLANES, :]
        carry_ref[lt] = jnp.broadcast_to(last, (SUBLANES, LANES))

    def scan_step(t, hs):
        rows = step_rows(t)
        out = []
        for lt in range(LRU_SLABS):
            h = a_ref[lt, rows, :] * hs[lt] + u_ref[lt, rows, :]
            h_ref[lt, rows, :] = h
            out.append(h)
        return tuple(out)

    lax.fori_loop(0, LRU_CHUNK, scan_step, tuple(h_in), unroll=LRU_UNROLL)


def _lru_fwd_kernel(x_ref, prev_ref, next_ref, cw_ref, cb_ref, wa_ref, ba_ref, wx_ref, bx_ref, lam_ref,
                    hf_ref, xc_ref, a_ref, u_ref, h_ref, carry_ref, *, n_tiles):
    i = pl.program_id(1)

    @pl.when(i == 0)
    def _():
        carry_ref[...] = jnp.zeros(carry_ref.shape, F32)

    has_prev = (i > 0).astype(F32)
    has_next = (i < n_tiles - 1).astype(F32)
    xc = _lru_conv(x_ref, prev_ref, next_ref, cw_ref, cb_ref, has_prev, has_next)
    xc_ref[...] = xc
    _lru_coeffs(xc, wa_ref, ba_ref, wx_ref, bx_ref, lam_ref, a_ref, u_ref)
    _scan_tile(a_ref, u_ref, h_ref, carry_ref, reverse=False)
    for lt in range(LRU_SLABS):
        hf_ref[:, lt * LANES:(lt + 1) * LANES] = h_ref[lt, 0:T_LRU, :]


def _lru_bwd_kernel(xc_ref, y_ref, hf_ref, wa_ref, ba_ref, wx_ref, bx_ref, lam_ref,
                    o_ref, a_ref, u_ref, h_ref, carry_ref):
    @pl.when(pl.program_id(1) == 0)
    def _():
        carry_ref[...] = jnp.zeros(carry_ref.shape, F32)

    _lru_coeffs(xc_ref[...], wa_ref, ba_ref, wx_ref, bx_ref, lam_ref, a_ref, u_ref)
    _scan_tile(a_ref, u_ref, h_ref, carry_ref, reverse=True)
    for lt in range(LRU_SLABS):
        cols = slice(lt * LANES, (lt + 1) * LANES)
        y = y_ref[:, cols].astype(F32)
        inner = y * (GELU_C + (GELU_C * 0.044715) * (y * y))
        gelu = y * (0.5 + 0.5 * jnp.tanh(inner))
        o_ref[:, cols] = ((hf_ref[:, cols] + h_ref[lt, 0:T_LRU, :]) * gelu).astype(BF16)


def _lru_call(xbr, ybr, conv_w, conv_b, wa, ba, wx, bx, lam):
    b, s, _ = xbr.shape
    n_tiles = s // T_LRU
    blocks_per_tile = T_LRU // SUBLANES
    n_blocks = n_tiles * blocks_per_tile
    half = D_LRU // 2
    slab = pltpu.VMEM((LRU_SLABS, LRU_PAD_ROWS, LANES), F32)
    scratch = [slab, slab, slab, pltpu.VMEM((LRU_SLABS, SUBLANES, LANES), F32)]
    params = pltpu.CompilerParams(vmem_limit_bytes=VMEM_LIMIT)
    c2 = lambda bi, i: (0, 0)
    c3 = lambda bi, i: (0, 0, 0)
    row_spec = pl.BlockSpec((1, D_LRU), c2)
    gate_specs = [pl.BlockSpec((2, half, half), c3), row_spec, pl.BlockSpec((2, half, half), c3),
                  row_spec, row_spec]

    fwd_tile = pl.BlockSpec((None, T_LRU, D_LRU), lambda bi, i: (bi, i, 0))
    prev = lambda bi, i: (bi, jnp.maximum(i * blocks_per_tile - 1, 0), 0)
    nxt = lambda bi, i: (bi, jnp.minimum((i + 1) * blocks_per_tile, n_blocks - 1), 0)
    h_fwd, xc = pl.pallas_call(
        functools.partial(_lru_fwd_kernel, n_tiles=n_tiles),
        grid=(b, n_tiles),
        in_specs=[fwd_tile, pl.BlockSpec((None, SUBLANES, D_LRU), prev),
                  pl.BlockSpec((None, SUBLANES, D_LRU), nxt),
                  pl.BlockSpec((CONV_W, D_LRU), c2), row_spec] + gate_specs,
        out_specs=[fwd_tile, fwd_tile],
        out_shape=[jax.ShapeDtypeStruct((b, s, D_LRU), F32), jax.ShapeDtypeStruct((b, s, D_LRU), F32)],
        scratch_shapes=scratch,
        compiler_params=params,
        name="lru_fwd",
    )(xbr, xbr, xbr, conv_w, conv_b, wa[0], ba[0:1], wx[0], bx[0:1], lam[0:1])

    bwd_tile = pl.BlockSpec((None, T_LRU, D_LRU), lambda bi, i: (bi, n_tiles - 1 - i, 0))
    return pl.pallas_call(
        _lru_bwd_kernel,
        grid=(b, n_tiles),
        in_specs=[bwd_tile, bwd_tile, bwd_tile] + gate_specs,
        out_specs=bwd_tile,
        out_shape=jax.ShapeDtypeStruct((b, s, D_LRU), BF16),
        scratch_shapes=scratch,
        compiler_params=params,
        name="lru_bwd",
    )(xc, ybr, h_fwd, wa[1], ba[1:2], wx[1], bx[1:2], lam[1:2])


def _mlp_kernel(x_ref, attn_ref, rec_ref, wo_ref, g2_ref, wup_ref, wdn_ref, gf_ref, o_ref):
    x1 = (x_ref[...]
          + jnp.dot(attn_ref[...], wo_ref[0:D_ATTN, :], preferred_element_type=F32)
          + jnp.dot(rec_ref[...], wo_ref[D_ATTN:, :], preferred_element_type=F32))
    h = (x1 * _rms_scale(x1) * g2_ref[...]).astype(BF16)
    mlp = None
    for c in range(D_FF // FF_CHUNK):
        up = jnp.dot(h, wup_ref[:, c * FF_CHUNK:(c + 1) * FF_CHUNK], preferred_element_type=F32)
        act = jnp.square(jnp.maximum(up, 0.0)).astype(BF16)
        down = jnp.dot(act, wdn_ref[c * FF_CHUNK:(c + 1) * FF_CHUNK, :], preferred_element_type=F32)
        mlp = down if mlp is None else mlp + down
    x2 = x1 + mlp
    o_ref[...] = x2 * _rms_scale(x2) * gf_ref[...]


def _mlp_call(x2d, attn2d, rec2d, w_out, g_mlp, w_up, w_down, g_final):
    n_tok = x2d.shape[0]
    const = lambda i: (0, 0)
    tok = lambda i: (i, 0)
    resident = pl.Buffered(1)
    return pl.pallas_call(
        _mlp_kernel,
        grid=(n_tok // TM_MLP,),
        in_specs=[
            pl.BlockSpec((TM_MLP, D_MODEL), tok),
            pl.BlockSpec((TM_MLP, D_ATTN), tok),
            pl.BlockSpec((TM_MLP, D_LRU), tok),
            pl.BlockSpec((D_MODEL, D_MODEL), const, pipeline_mode=resident),
            pl.BlockSpec((1, D_MODEL), const),
            pl.BlockSpec((D_MODEL, D_FF), const, pipeline_mode=resident),
            pl.BlockSpec((D_FF, D_MODEL), const, pipeline_mode=resident),
            pl.BlockSpec((1, D_MODEL), const),
        ],
        out_specs=pl.BlockSpec((TM_MLP, D_MODEL), tok),
        out_shape=jax.ShapeDtypeStruct((n_tok, D_MODEL), F32),
        compiler_params=pltpu.CompilerParams(vmem_limit_bytes=VMEM_LIMIT),
        name="mlp",
    )(x2d, attn2d, rec2d, w_out, g_mlp, w_up, w_down, g_final)


def _rope_tables(seq_len):
    rows = seq_len // GRID_W
    row_ids = jnp.repeat(jnp.arange(rows), GRID_W).astype(F32)
    col_ids = jnp.tile(jnp.arange(GRID_W), rows).astype(F32)
    inv_freq = ROPE_THETA ** (-jnp.arange(0, ROPE_HALF, 2, dtype=F32) / ROPE_HALF)
    ang_r = row_ids[:, None] * inv_freq
    ang_c = col_ids[:, None] * inv_freq
    cos = jnp.concatenate([jnp.cos(ang_r)] * 2 + [jnp.cos(ang_c)] * 2, axis=-1)
    sin = jnp.concatenate([-jnp.sin(ang_r), jnp.sin(ang_r), -jnp.sin(ang_c), jnp.sin(ang_c)], axis=-1)
    return jnp.tile(cos, (1, LANES // HEAD_DIM)), jnp.tile(sin, (1, LANES // HEAD_DIM))


def _block_diag_halves(w):
    per_half = N_LRU_BLOCKS // 2
    halves = [jax.scipy.linalg.block_diag(*[w[c * per_half + n] for n in range(per_half)])
              for c in range(2)]
    return jnp.stack(halves).astype(BF16)


def _layer(x, params, seq_tables):
    b, s, _ = x.shape
    x2d = x.reshape(b * s, D_MODEL)
    q, kd, va, xbr, ybr = _proj_call(x2d, params["g_mix"], params["w_in"], params["gqk"],
                                     params["seg"], seq_tables[0], seq_tables[1], s)
    attn = _attention(q.reshape(b, s, D_ATTN), kd.reshape(b, s, D_KDUP), va.reshape(b, s, D_KDUP),
                      params["score_bound"])
    rec = _lru_call(xbr.reshape(b, s, D_LRU), ybr.reshape(b, s, D_LRU), params["conv_w"],
                    params["conv_b"], params["wa"], params["ba"], params["wx"], params["bx"],
                    params["lam"])
    y = _mlp_call(x2d, attn.reshape(b * s, D_ATTN), rec.reshape(b * s, D_LRU), params["w_out"],
                  params["g_mlp"], params["w_up"], params["w_down"], params["g_final"])
    return y.reshape(b, s, D_MODEL)


def kernel(x_prompt, x_sample, norm_mix_g, w_in, q_norm_g, k_norm_g, conv_w, conv_b, lru_wa, lru_ba,
           lru_wx, lru_bx, lru_lambda, w_out, norm_mlp_g, w_up, w_down, norm_final_g):
    l = 0
    q_gain = q_norm_g[l] * (HEAD_DIM ** -0.5 * math.log2(math.e))
    gqk = jnp.concatenate([jnp.tile(q_gain, N_Q_HEADS), jnp.tile(k_norm_g[l], N_KV_HEADS)])
    seg = jax.scipy.linalg.block_diag(*[jnp.ones((HEAD_DIM, HEAD_DIM), F32)] * (MXU_DIM // HEAD_DIM))
    score_bound = (HEAD_DIM * SCORE_BOUND_SLACK) * jnp.max(jnp.abs(q_gain)) * jnp.max(jnp.abs(k_norm_g[l]))
    params = {
        "score_bound": score_bound,
        "g_mix": norm_mix_g[l][None, :],
        "w_in": w_in[l].astype(BF16),
        "gqk": gqk[None, :],
        "seg": seg.astype(BF16),
        "conv_w": conv_w[l],
        "conv_b": conv_b[l][None, :],
        "wa": jnp.stack([_block_diag_halves(0.5 * lru_wa[l, d]) for d in range(2)]),
        "ba": 0.5 * lru_ba[l],
        "wx": jnp.stack([_block_diag_halves(0.5 * lru_wx[l, d]) for d in range(2)]),
        "bx": 0.5 * lru_bx[l],
        "lam": lru_lambda[l],
        "w_out": w_out[l].astype(BF16),
        "g_mlp": norm_mlp_g[l][None, :],
        "w_up": w_up[l].astype(BF16),
        "w_down": w_down[l].astype(BF16),
        "g_final": norm_final_g[None, :],
    }
    outs = []
    for x in (x_prompt, x_sample):
        outs.append(_layer(x, params, _rope_tables(x.shape[1])))
    return tuple(outs)
```

```python
import functools
import math

import jax
import jax.numpy as jnp
from jax import lax
from jax.experimental import pallas as pl
from jax.experimental.pallas import tpu as pltpu

F32 = jnp.float32
BF16 = jnp.bfloat16

D_MODEL = 1024
HEAD_DIM = 64
N_Q_HEADS = 8
N_KV_HEADS = 2
Q_PER_KV = N_Q_HEADS // N_KV_HEADS
D_ATTN = N_Q_HEADS * HEAD_DIM
D_KV = N_KV_HEADS * HEAD_DIM
D_LRU = 512
N_LRU_BLOCKS = 8
LRU_BLOCK = D_LRU // N_LRU_BLOCKS
LRU_C = 8.0
CONV_W = 4
D_FF = 4 * D_MODEL
GRID_W = 64
ROPE_HALF = HEAD_DIM // 2
ROPE_QUARTER = ROPE_HALF // 2
ROPE_THETA = 10000.0
EPS = 1e-6
GELU_C = math.sqrt(2.0 / math.pi)

LANES = 128
SUBLANES = 8
MXU_DIM = 256

D_PROJ = D_ATTN + 2 * D_KV + 2 * D_LRU
D_QK = D_ATTN + D_KV
D_KDUP = 2 * D_KV

TM_PROJ = 1024
TQ = 512
TQ_ONE_STEP = 1024
TK_STEP = 4096
TK_CHUNK = 512
T_LRU = 2048
LRU_CHUNK = T_LRU // SUBLANES + 4
LRU_PAD_ROWS = SUBLANES * LRU_CHUNK
LRU_SLABS = D_LRU // LANES
LRU_UNROLL = 20
TM_MLP = 512
FF_CHUNK = 1024
VMEM_LIMIT = 56 * 1024 * 1024

NEG_BIG = -1e30
MAX_SHIFTED_RANGE = 100.0
SCORE_BOUND_SLACK = 1.0 + 2.0 ** -6


def _rms_scale(x):
    return lax.rsqrt(jnp.mean(x * x, axis=-1, keepdims=True) + EPS)


def _proj_kernel(x_ref, g_ref, w_ref, gqk_ref, seg_ref, cos_ref, sin_ref,
                 q_ref, k_ref, v_ref, xbr_ref, ybr_ref):
    x = x_ref[...]
    proj = jnp.dot((x * g_ref[...]).astype(BF16), w_ref[...], preferred_element_type=F32)
    proj = proj * _rms_scale(x)

    cos = cos_ref[...]
    sin = sin_ref[...]
    lane = lax.broadcasted_iota(jnp.int32, cos.shape, 1)
    first_half = (lane & (ROPE_HALF - 1)) < ROPE_QUARTER
    low_half = lane < HEAD_DIM

    def norm_rope(t, seg, gain):
        ssum = jnp.dot((t * t).astype(BF16), seg, preferred_element_type=F32)
        tn = t * lax.rsqrt(ssum * (1.0 / HEAD_DIM) + EPS) * gain
        out = []
        for s in range(t.shape[1] // LANES):
            u = tn[:, s * LANES:(s + 1) * LANES]
            partner = jnp.where(first_half,
                                pltpu.roll(u, LANES - ROPE_QUARTER, 1),
                                pltpu.roll(u, ROPE_QUARTER, 1))
            out.append(u * cos + partner * sin)
        return out

    for c in range(D_ATTN // MXU_DIM):
        cols = slice(c * MXU_DIM, (c + 1) * MXU_DIM)
        for s, r in enumerate(norm_rope(proj[:, cols], seg_ref[...], gqk_ref[:, cols])):
            q_ref[:, c * MXU_DIM + s * LANES:c * MXU_DIM + (s + 1) * LANES] = r.astype(BF16)

    (k01,) = norm_rope(proj[:, D_ATTN:D_ATTN + D_KV], seg_ref[0:LANES, 0:LANES],
                       gqk_ref[:, D_ATTN:D_ATTN + D_KV])
    unit_lane = (lane == HEAD_DIM).astype(F32)
    k_ref[:, 0:LANES] = jnp.where(low_half, k01, unit_lane).astype(BF16)
    k_ref[:, LANES:] = jnp.where(low_half, pltpu.roll(k01, HEAD_DIM, 1), unit_lane).astype(BF16)
    v01 = proj[:, D_ATTN + D_KV:D_ATTN + 2 * D_KV]
    v_ref[:, 0:LANES] = jnp.where(low_half, v01, 1.0).astype(BF16)
    v_ref[:, LANES:] = jnp.where(low_half, pltpu.roll(v01, HEAD_DIM, 1), 1.0).astype(BF16)
    xbr_ref[...] = proj[:, D_ATTN + 2 * D_KV:D_ATTN + 2 * D_KV + D_LRU].astype(BF16)
    ybr_ref[...] = proj[:, D_ATTN + 2 * D_KV + D_LRU:].astype(BF16)


def _proj_call(x2d, g_mix, w_in, gqk, seg, cos_t, sin_t, seq_len):
    n_tok = x2d.shape[0]
    tiles_per_seq = seq_len // TM_PROJ
    const = lambda i: (0, 0)
    tok = lambda i: (i, 0)
    pos = lambda i: (i % tiles_per_seq, 0)
    return pl.pallas_call(
        _proj_kernel,
        grid=(n_tok // TM_PROJ,),
        in_specs=[
            pl.BlockSpec((TM_PROJ, D_MODEL), tok),
            pl.BlockSpec((1, D_MODEL), const),
            pl.BlockSpec((D_MODEL, D_PROJ), const),
            pl.BlockSpec((1, D_QK), const),
            pl.BlockSpec((MXU_DIM, MXU_DIM), const),
            pl.BlockSpec((TM_PROJ, LANES), pos),
            pl.BlockSpec((TM_PROJ, LANES), pos),
        ],
        out_specs=[
            pl.BlockSpec((TM_PROJ, D_ATTN), tok),
            pl.BlockSpec((TM_PROJ, D_KDUP), tok),
            pl.BlockSpec((TM_PROJ, D_KDUP), tok),
            pl.BlockSpec((TM_PROJ, D_LRU), tok),
            pl.BlockSpec((TM_PROJ, D_LRU), tok),
        ],
        out_shape=[
            jax.ShapeDtypeStruct((n_tok, D_ATTN), BF16),
            jax.ShapeDtypeStruct((n_tok, D_KDUP), BF16),
            jax.ShapeDtypeStruct((n_tok, D_KDUP), BF16),
            jax.ShapeDtypeStruct((n_tok, D_LRU), BF16),
            jax.ShapeDtypeStruct((n_tok, D_LRU), BF16),
        ],
        compiler_params=pltpu.CompilerParams(vmem_limit_bytes=VMEM_LIMIT),
        name="proj",
    )(x2d, g_mix, w_in, gqk, seg, cos_t, sin_t)


def _attn_kernel(flag_ref, q_ref, k_ref, v_ref, shift_ref, o_ref, qs_ref, acc_ref, m_ref, *, n_kv):
    j = pl.program_id(3)
    bounded = flag_ref[0] != 0
    tq, tk_step = q_ref.shape[0], k_ref.shape[0]
    ts = min(tq, TQ)
    rows_s = Q_PER_KV * ts
    lane = lax.broadcasted_iota(jnp.int32, (ts, LANES), 1)
    low_half = lane < HEAD_DIM
    chunks = [slice(c * TK_CHUNK, (c + 1) * TK_CHUNK) for c in range(tk_step // TK_CHUNK)]

    def stack_queries(sub):
        for hh in range(Q_PER_KV):
            pair = q_ref[sub * ts:(sub + 1) * ts, (hh // 2) * LANES:(hh // 2 + 1) * LANES].astype(F32)
            if hh % 2:
                pair = pltpu.roll(pair, HEAD_DIM, 1)
            base = sub * rows_s + hh * ts
            qs_ref[base:base + ts, :] = jnp.where(low_half, pair, shift_ref[...]).astype(BF16)

    def scores(sub, keys):
        return lax.dot_general(qs_ref[sub * rows_s:(sub + 1) * rows_s, :], k_ref[keys, :],
                               (((1,), (1,)), ((), ())),
                               preferred_element_type=F32)

    def bounded_chunks(sub, first_assigns):
        acc_rows = slice(sub * rows_s, (sub + 1) * rows_s)
        for c, keys in enumerate(chunks):
            p = jnp.exp2(scores(sub, keys)).astype(BF16)
            part = jnp.dot(p, v_ref[keys, :], preferred_element_type=F32)
            if first_assigns and c == 0:
                acc_ref[acc_rows, :] = part
            else:
                acc_ref[acc_rows, :] += part

    def online_chunks(sub):
        acc_rows = slice(sub * rows_s, (sub + 1) * rows_s)
        m = m_ref[acc_rows, :]
        acc = acc_ref[acc_rows, :]
        for keys in chunks:
            s = scores(sub, keys)
            m_new = jnp.maximum(m, jnp.max(s, axis=1, keepdims=True))
            alpha = jnp.exp2(m - m_new)
            p = jnp.exp2(s - jnp.tile(m_new, (1, TK_CHUNK // LANES))).astype(BF16)
            acc = alpha * acc + jnp.dot(p, v_ref[keys, :], preferred_element_type=F32)
            m = m_new
        m_ref[acc_rows, :] = m
        acc_ref[acc_rows, :] = acc

    def init_online():
        acc_ref[...] = jnp.zeros(acc_ref.shape, F32)
        m_ref[...] = jnp.full(m_ref.shape, NEG_BIG, F32)

    def finalize(sub):
        a = acc_ref[sub * rows_s:(sub + 1) * rows_s, :]
        swapped = pltpu.roll(a, HEAD_DIM, 1)
        for pr in range(Q_PER_KV // 2):
            ev = slice((2 * pr) * ts, (2 * pr + 1) * ts)
            od = slice((2 * pr + 1) * ts, (2 * pr + 2) * ts)
            num = jnp.where(low_half, a[ev], swapped[od])
            den = jnp.where(low_half, swapped[ev], a[od])
            o_ref[sub * ts:(sub + 1) * ts, pr * LANES:(pr + 1) * LANES] = (num / den).astype(BF16)

    subs = range(tq // ts)
    if n_kv == 1:
        @pl.when(bounded)
        def _():
            for sub in subs:
                stack_queries(sub)
                bounded_chunks(sub, first_assigns=True)
                finalize(sub)

        @pl.when(jnp.logical_not(bounded))
        def _():
            init_online()
            for sub in subs:
                stack_queries(sub)
                online_chunks(sub)
                finalize(sub)
    else:
        @pl.when(j == 0)
        def _():
            for sub in subs:
                stack_queries(sub)
            init_online()

        @pl.when(bounded)
        def _():
            for sub in subs:
                bounded_chunks(sub, first_assigns=False)

        @pl.when(jnp.logical_not(bounded))
        def _():
            for sub in subs:
                online_chunks(sub)

        @pl.when(j == n_kv - 1)
        def _():
            for sub in subs:
                finalize(sub)


def _attention(q, ka, va, score_bound):
    b, s, _ = q.shape
    tk_step = min(s, TK_STEP)
    n_kv = s // tk_step
    tq = min(s, TQ_ONE_STEP if n_kv == 1 else TQ)
    n_q = s // tq
    rows = Q_PER_KV * tq
    use_bounded = 2.0 * score_bound <= MAX_SHIFTED_RANGE
    flag = use_bounded.astype(jnp.int32).reshape(1)
    shift = jnp.zeros((1, LANES), F32).at[0, HEAD_DIM].set(jnp.where(use_bounded, -score_bound, 0.0))
    return pl.pallas_call(
        functools.partial(_attn_kernel, n_kv=n_kv),
        grid=(b, N_KV_HEADS, n_q, n_kv),
        in_specs=[
            pl.BlockSpec(memory_space=pltpu.SMEM),
            pl.BlockSpec((None, tq, Q_PER_KV * HEAD_DIM), lambda bi, g, i, j: (bi, i, g)),
            pl.BlockSpec((None, tk_step, LANES), lambda bi, g, i, j: (bi, j, g)),
            pl.BlockSpec((None, tk_step, LANES), lambda bi, g, i, j: (bi, j, g)),
            pl.BlockSpec((1, LANES), lambda bi, g, i, j: (0, 0)),
        ],
        out_specs=pl.BlockSpec((None, tq, Q_PER_KV * HEAD_DIM), lambda bi, g, i, j: (bi, i, g)),
        out_shape=jax.ShapeDtypeStruct((b, s, D_ATTN), BF16),
        scratch_shapes=[
            pltpu.VMEM((rows, LANES), BF16),
            pltpu.VMEM((rows, LANES), F32),
            pltpu.VMEM((rows, LANES), F32),
        ],
        compiler_params=pltpu.CompilerParams(vmem_limit_bytes=VMEM_LIMIT),
        name="attn",
    )(flag, q, ka, va, shift)


def _lru_conv(x_ref, prev_ref, next_ref, cw_ref, cb_ref, has_prev, has_next):
    t_rows = x_ref.shape[0]
    x = x_ref[...].astype(F32)
    prev = prev_ref[...].astype(F32) * has_prev
    nxt = next_ref[...].astype(F32) * has_next
    row8 = lax.broadcasted_iota(jnp.int32, (SUBLANES, D_LRU), 0)

    def shifted(k):
        if k == 0:
            return x
        rolled = pltpu.roll(x, (-k) % t_rows, 0)
        if k < 0:
            head = jnp.where(row8 < -k, pltpu.roll(prev, -k, 0), rolled[0:SUBLANES])
            return jnp.concatenate([head, rolled[SUBLANES:]], axis=0)
        tail = jnp.where(row8 >= SUBLANES - k, pltpu.roll(nxt, SUBLANES - k, 0), rolled[t_rows - SUBLANES:])
        return jnp.concatenate([rolled[:t_rows - SUBLANES], tail], axis=0)

    xc = cb_ref[...]
    for w in range(CONV_W):
        xc = xc + cw_ref[w:w + 1, :] * shifted(w - 2)
    return xc


def _lru_coeffs(xc, wa_ref, ba_ref, wx_ref, bx_ref, lam_ref, a_ref, u_ref):
    t_rows = xc.shape[0]
    xcb = xc.astype(BF16)
    half = D_LRU // 2
    r_parts, i_parts = [], []
    for c in range(2):
        xh = xcb[:, c * half:(c + 1) * half]
        r_parts.append(jnp.dot(xh, wa_ref[c], preferred_element_type=F32))
        i_parts.append(jnp.dot(xh, wx_ref[c], preferred_element_type=F32))
    t_r = jnp.tanh(jnp.concatenate(r_parts, axis=1) + ba_ref[...])
    t_i = jnp.tanh(jnp.concatenate(i_parts, axis=1) + bx_ref[...])
    lam = lam_ref[...]
    neg_softplus = -(jnp.maximum(-lam, 0.0) + jnp.log1p(jnp.exp(-jnp.abs(lam))))
    log_a = (0.5 * LRU_C * neg_softplus) * (1.0 + t_r)
    a = jnp.exp(log_a)
    th = jnp.tanh(log_a)
    u = jnp.exp2(0.5 * jnp.log2(-0.5 * th / (1.0 - th))) * ((1.0 + t_i) * xc)
    for lt in range(LRU_SLABS):
        a_ref[lt, 0:t_rows, :] = a[:, lt * LANES:(lt + 1) * LANES]
        u_ref[lt, 0:t_rows, :] = u[:, lt * LANES:(lt + 1) * LANES]
        a_ref[lt, t_rows:, :] = jnp.ones((LRU_PAD_ROWS - t_rows, LANES), F32)
        u_ref[lt, t_rows:, :] = jnp.zeros((LRU_PAD_ROWS - t_rows, LANES), F32)


def _sublane_shift(x, k, fill, row, reverse):
    if reverse:
        return jnp.where(row < SUBLANES - k, pltpu.roll(x, SUBLANES - k, 0), fill)
    return jnp.where(row >= k, pltpu.roll(x, k, 0), fill)


def _scan_tile(a_ref, u_ref, h_ref, carry_ref, reverse):
    row = lax.broadcasted_iota(jnp.int32, (SUBLANES, LANES), 0)

    n_blocks = LRU_CHUNK // LRU_UNROLL

    def block_rows(blk):
        if reverse:
            base = (n_blocks - 1 - blk) * LRU_UNROLL
            offsets = range(LRU_UNROLL - 1, -1, -1)
        else:
            base = blk * LRU_UNROLL
            offsets = range(LRU_UNROLL)
        return [pl.ds(base + k, SUBLANES, stride=LRU_CHUNK) for k in offsets]

    def reduce_block(blk, au):
        acc_a, acc_u = list(au[0]), list(au[1])
        for rows in block_rows(blk):
            for lt in range(LRU_SLABS):
                a_t = a_ref[lt, rows, :]
                acc_u[lt] = a_t * acc_u[lt] + u_ref[lt, rows, :]
                acc_a[lt] = a_t * acc_a[lt]
        return tuple(acc_a), tuple(acc_u)

    ones = tuple(jnp.ones((SUBLANES, LANES), F32) for _ in range(LRU_SLABS))
    zeros = tuple(jnp.zeros((SUBLANES, LANES), F32) for _ in range(LRU_SLABS))
    tot_a, tot_u = lax.fori_loop(0, n_blocks, reduce_block, (ones, zeros))

    h_in = []
    for lt in range(LRU_SLABS):
        a, u = tot_a[lt], tot_u[lt]
        for k in (1, 2, 4):
            u = u + a * _sublane_shift(u, k, 0.0, row, reverse)
            a = a * _sublane_shift(a, k, 1.0, row, reverse)
        carry = carry_ref[lt]
        h_out = u + a * carry
        h_in.append(_sublane_shift(h_out, 1, carry, row, reverse))
        last = h_out[0:1, :] if reverse else h_out[SUBLANES - 1:SUBLANES, :]
        carry_ref[lt] = jnp.broadcast_to(last, (SUBLANES, LANES))

    def scan_block(blk, hs):
        hs = list(hs)
        for rows in block_rows(blk):
            for lt in range(LRU_SLABS):
                hs[lt] = a_ref[lt, rows, :] * hs[lt] + u_ref[lt, rows, :]
                h_ref[lt, rows, :] = hs[lt]
        return tuple(hs)

    lax.fori_loop(0, n_blocks, scan_block, tuple(h_in))


def _lru_fwd_kernel(x_ref, prev_ref, next_ref, cw_ref, cb_ref, wa_ref, ba_ref, wx_ref, bx_ref, lam_ref,
                    hf_ref, xc_ref, a_ref, u_ref, h_ref, carry_ref, *, n_tiles):
    i = pl.program_id(1)

    @pl.when(i == 0)
    def _():
        carry_ref[...] = jnp.zeros(carry_ref.shape, F32)

    has_prev = (i > 0).astype(F32)
    has_next = (i < n_tiles - 1).astype(F32)
    xc = _lru_conv(x_ref, prev_ref, next_ref, cw_ref, cb_ref, has_prev, has_next)
    xc_ref[...] = xc
    _lru_coeffs(xc, wa_ref, ba_ref, wx_ref, bx_ref, lam_ref, a_ref, u_ref)
    _scan_tile(a_ref, u_ref, h_ref, carry_ref, reverse=False)
    for lt in range(LRU_SLABS):
        hf_ref[:, lt * LANES:(lt + 1) * LANES] = h_ref[lt, 0:T_LRU, :]


def _lru_bwd_kernel(xc_ref, y_ref, hf_ref, wa_ref, ba_ref, wx_ref, bx_ref, lam_ref,
                    o_ref, a_ref, u_ref, h_ref, carry_ref):
    @pl.when(pl.program_id(1) == 0)
    def _():
        carry_ref[...] = jnp.zeros(carry_ref.shape, F32)

    _lru_coeffs(xc_ref[...], wa_ref, ba_ref, wx_ref, bx_ref, lam_ref, a_ref, u_ref)
    _scan_tile(a_ref, u_ref, h_ref, carry_ref, reverse=True)
    for lt in range(LRU_SLABS):
        cols = slice(lt * LANES, (lt + 1) * LANES)
        y = y_ref[:, cols].astype(F32)
        inner = y * (GELU_C + (GELU_C * 0.044715) * (y * y))
        gelu = y * (0.5 + 0.5 * jnp.tanh(inner))
        o_ref[:, cols] = ((hf_ref[:, cols] + h_ref[lt, 0:T_LRU, :]) * gelu).astype(BF16)


def _lru_call(xbr, ybr, conv_w, conv_b, wa, ba, wx, bx, lam):
    b, s, _ = xbr.shape
    n_tiles = s // T_LRU
    blocks_per_tile = T_LRU // SUBLANES
    n_blocks = n_tiles * blocks_per_tile
    half = D_LRU // 2
    slab = pltpu.VMEM((LRU_SLABS, LRU_PAD_ROWS, LANES), F32)
    scratch = [slab, slab, slab, pltpu.VMEM((LRU_SLABS, SUBLANES, LANES), F32)]
    params = pltpu.CompilerParams(vmem_limit_bytes=VMEM_LIMIT)
    c2 = lambda bi, i: (0, 0)
    c3 = lambda bi, i: (0, 0, 0)
    row_spec = pl.BlockSpec((1, D_LRU), c2)
    gate_specs = [pl.BlockSpec((2, half, half), c3), row_spec, pl.BlockSpec((2, half, half), c3),
                  row_spec, row_spec]

    fwd_tile = pl.BlockSpec((None, T_LRU, D_LRU), lambda bi, i: (bi, i, 0))
    prev = lambda bi, i: (bi, jnp.maximum(i * blocks_per_tile - 1, 0), 0)
    nxt = lambda bi, i: (bi, jnp.minimum((i + 1) * blocks_per_tile, n_blocks - 1), 0)
    h_fwd, xc = pl.pallas_call(
        functools.partial(_lru_fwd_kernel, n_tiles=n_tiles),
        grid=(b, n_tiles),
        in_specs=[fwd_tile, pl.BlockSpec((None, SUBLANES, D_LRU), prev),
                  pl.BlockSpec((None, SUBLANES, D_LRU), nxt),
                  pl.BlockSpec((CONV_W, D_LRU), c2), row_spec] + gate_specs,
        out_specs=[fwd_tile, fwd_tile],
        out_shape=[jax.ShapeDtypeStruct((b, s, D_LRU), F32), jax.ShapeDtypeStruct((b, s, D_LRU), F32)],
        scratch_shapes=scratch,
        compiler_params=params,
        name="lru_fwd",
    )(xbr, xbr, xbr, conv_w, conv_b, wa[0], ba[0:1], wx[0], bx[0:1], lam[0:1])

    bwd_tile = pl.BlockSpec((None, T_LRU, D_LRU), lambda bi, i: (bi, n_tiles - 1 - i, 0))
    return pl.pallas_call(
        _lru_bwd_kernel,
        grid=(b, n_tiles),
        in_specs=[bwd_tile, bwd_tile, bwd_tile] + gate_specs,
        out_specs=bwd_tile,
        out_shape=jax.ShapeDtypeStruct((b, s, D_LRU), BF16),
        scratch_shapes=scratch,
        compiler_params=params,
        name="lru_bwd",
    )(xc, ybr, h_fwd, wa[1], ba[1:2], wx[1], bx[1:2], lam[1:2])


def _mlp_kernel(x_ref, attn_ref, rec_ref, wo_ref, g2_ref, wup_ref, wdn_ref, gf_ref, o_ref):
    x1 = (x_ref[...]
          + jnp.dot(attn_ref[...], wo_ref[0:D_ATTN, :], preferred_element_type=F32)
          + jnp.dot(rec_ref[...], wo_ref[D_ATTN:, :], preferred_element_type=F32))
    h = (x1 * _rms_scale(x1) * g2_ref[...]).astype(BF16)
    mlp = None
    for c in range(D_FF // FF_CHUNK):
        up = jnp.dot(h, wup_ref[:, c * FF_CHUNK:(c + 1) * FF_CHUNK], preferred_element_type=F32)
        act = jnp.square(jnp.maximum(up, 0.0)).astype(BF16)
        down = jnp.dot(act, wdn_ref[c * FF_CHUNK:(c + 1) * FF_CHUNK, :], preferred_element_type=F32)
        mlp = down if mlp is None else mlp + down
    x2 = x1 + mlp
    o_ref[...] = x2 * _rms_scale(x2) * gf_ref[...]


def _mlp_call(x2d, attn2d, rec2d, w_out, g_mlp, w_up, w_down, g_final):
    n_tok = x2d.shape[0]
    const = lambda i: (0, 0)
    tok = lambda i: (i, 0)
    resident = pl.Buffered(1)
    return pl.pallas_call(
        _mlp_kernel,
        grid=(n_tok // TM_MLP,),
        in_specs=[
            pl.BlockSpec((TM_MLP, D_MODEL), tok),
            pl.BlockSpec((TM_MLP, D_ATTN), tok),
            pl.BlockSpec((TM_MLP, D_LRU), tok),
            pl.BlockSpec((D_MODEL, D_MODEL), const, pipeline_mode=resident),
            pl.BlockSpec((1, D_MODEL), const),
            pl.BlockSpec((D_MODEL, D_FF), const, pipeline_mode=resident),
            pl.BlockSpec((D_FF, D_MODEL), const, pipeline_mode=resident),
            pl.BlockSpec((1, D_MODEL), const),
        ],
        out_specs=pl.BlockSpec((TM_MLP, D_MODEL), tok),
        out_shape=jax.ShapeDtypeStruct((n_tok, D_MODEL), F32),
        compiler_params=pltpu.CompilerParams(vmem_limit_bytes=VMEM_LIMIT),
        name="mlp",
    )(x2d, attn2d, rec2d, w_out, g_mlp, w_up, w_down, g_final)


def _rope_tables(seq_len):
    rows = seq_len // GRID_W
    row_ids = jnp.repeat(jnp.arange(rows), GRID_W).astype(F32)
    col_ids = jnp.tile(jnp.arange(GRID_W), rows).astype(F32)
    inv_freq = ROPE_THETA ** (-jnp.arange(0, ROPE_HALF, 2, dtype=F32) / ROPE_HALF)
    ang_r = row_ids[:, None] * inv_freq
    ang_c = col_ids[:, None] * inv_freq
    cos = jnp.concatenate([jnp.cos(ang_r)] * 2 + [jnp.cos(ang_c)] * 2, axis=-1)
    sin = jnp.concatenate([-jnp.sin(ang_r), jnp.sin(ang_r), -jnp.sin(ang_c), jnp.sin(ang_c)], axis=-1)
    return jnp.tile(cos, (1, LANES // HEAD_DIM)), jnp.tile(sin, (1, LANES // HEAD_DIM))


def _block_diag_halves(w):
    per_half = N_LRU_BLOCKS // 2
    halves = [jax.scipy.linalg.block_diag(*[w[c * per_half + n] for n in range(per_half)])
              for c in range(2)]
    return jnp.stack(halves).astype(BF16)


def _layer(x, params, seq_tables):
    b, s, _ = x.shape
    x2d = x.reshape(b * s, D_MODEL)
    q, kd, va, xbr, ybr = _proj_call(x2d, params["g_mix"], params["w_in"], params["gqk"],
                                     params["seg"], seq_tables[0], seq_tables[1], s)
    attn = _attention(q.reshape(b, s, D_ATTN), kd.reshape(b, s, D_KDUP), va.reshape(b, s, D_KDUP),
                      params["score_bound"])
    rec = _lru_call(xbr.reshape(b, s, D_LRU), ybr.reshape(b, s, D_LRU), params["conv_w"],
                    params["conv_b"], params["wa"], params["ba"], params["wx"], params["bx"],
                    params["lam"])
    y = _mlp_call(x2d, attn.reshape(b * s, D_ATTN), rec.reshape(b * s, D_LRU), params["w_out"],
                  params["g_mlp"], params["w_up"], params["w_down"], params["g_final"])
    return y.reshape(b, s, D_MODEL)


def kernel(x_prompt, x_sample, norm_mix_g, w_in, q_norm_g, k_norm_g, conv_w, conv_b, lru_wa, lru_ba,
           lru_wx, lru_bx, lru_lambda, w_out, norm_mlp_g, w_up, w_down, norm_final_g):
    l = 0
    q_gain = q_norm_g[l] * (HEAD_DIM ** -0.5 * math.log2(math.e))
    gqk = jnp.concatenate([jnp.tile(q_gain, N_Q_HEADS), jnp.tile(k_norm_g[l], N_KV_HEADS)])
    seg = jax.scipy.linalg.block_diag(*[jnp.ones((HEAD_DIM, HEAD_DIM), F32)] * (MXU_DIM // HEAD_DIM))
    score_bound = (HEAD_DIM * SCORE_BOUND_SLACK) * jnp.max(jnp.abs(q_gain)) * jnp.max(jnp.abs(k_norm_g[l]))
    params = {
        "score_bound": score_bound,
        "g_mix": norm_mix_g[l][None, :],
        "w_in": w_in[l].astype(BF16),
        "gqk": gqk[None, :],
        "seg": seg.astype(BF16),
        "conv_w": conv_w[l],
        "conv_b": conv_b[l][None, :],
        "wa": jnp.stack([_block_diag_halves(0.5 * lru_wa[l, d]) for d in range(2)]),
        "ba": 0.5 * lru_ba[l],
        "wx": jnp.stack([_block_diag_halves(0.5 * lru_wx[l, d]) for d in range(2)]),
        "bx": 0.5 * lru_bx[l],
        "lam": lru_lambda[l],
        "w_out": w_out[l].astype(BF16),
        "g_mlp": norm_mlp_g[l][None, :],
        "w_up": w_up[l].astype(BF16),
        "w_down": w_down[l].astype(BF16),
        "g_final": norm_final_g[None, :],
    }
    outs = []
    for x in (x_prompt, x_sample):
        outs.append(_layer(x, params, _rope_tables(x.shape[1])))
    return tuple(outs)
```

```python
import functools
import math

import jax
import jax.numpy as jnp
from jax import lax
from jax.experimental import pallas as pl
from jax.experimental.pallas import tpu as pltpu

F32 = jnp.float32
BF16 = jnp.bfloat16

D_MODEL = 1024
HEAD_DIM = 64
N_Q_HEADS = 8
N_KV_HEADS = 2
Q_PER_KV = N_Q_HEADS // N_KV_HEADS
D_ATTN = N_Q_HEADS * HEAD_DIM
D_KV = N_KV_HEADS * HEAD_DIM
D_LRU = 512
N_LRU_BLOCKS = 8
LRU_BLOCK = D_LRU // N_LRU_BLOCKS
LRU_C = 8.0
CONV_W = 4
D_FF = 4 * D_MODEL
GRID_W = 64
ROPE_HALF = HEAD_DIM // 2
ROPE_QUARTER = ROPE_HALF // 2
ROPE_THETA = 10000.0
EPS = 1e-6
GELU_C = math.sqrt(2.0 / math.pi)

LANES = 128
SUBLANES = 8
MXU_DIM = 256

D_PROJ = D_ATTN + 2 * D_KV + 2 * D_LRU
D_QK = D_ATTN + D_KV
D_KDUP = 2 * D_KV

TM_PROJ = 1024
TQ = 512
TQ_ONE_STEP = 1024
TK_STEP = 4096
TK_CHUNK = 512
T_LRU = 2048
LRU_CHUNK = T_LRU // SUBLANES + 4
LRU_PAD_ROWS = SUBLANES * LRU_CHUNK
LRU_SLABS = D_LRU // LANES
LRU_UNROLL = 20
TM_MLP = 512
FF_CHUNK = 1024
VMEM_LIMIT = 56 * 1024 * 1024

NEG_BIG = -1e30
MAX_SHIFTED_RANGE = 100.0
SCORE_BOUND_SLACK = 1.0 + 2.0 ** -6


def _rms_scale(x):
    return lax.rsqrt(jnp.mean(x * x, axis=-1, keepdims=True) + EPS)


def _proj_kernel(x_ref, g_ref, w_ref, gqk_ref, seg_ref, cos_ref, sin_ref,
                 q_ref, k_ref, v_ref, xbr_ref, ybr_ref):
    x = x_ref[...]
    proj = jnp.dot((x * g_ref[...]).astype(BF16), w_ref[...], preferred_element_type=F32)
    proj = proj * _rms_scale(x)

    cos = cos_ref[...]
    sin = sin_ref[...]
    lane = lax.broadcasted_iota(jnp.int32, cos.shape, 1)
    first_half = (lane & (ROPE_HALF - 1)) < ROPE_QUARTER
    low_half = lane < HEAD_DIM

    def norm_rope(t, seg, gain):
        ssum = jnp.dot((t * t).astype(BF16), seg, preferred_element_type=F32)
        tn = t * lax.rsqrt(ssum * (1.0 / HEAD_DIM) + EPS) * gain
        out = []
        for s in range(t.shape[1] // LANES):
            u = tn[:, s * LANES:(s + 1) * LANES]
            partner = jnp.where(first_half,
                                pltpu.roll(u, LANES - ROPE_QUARTER, 1),
                                pltpu.roll(u, ROPE_QUARTER, 1))
            out.append(u * cos + partner * sin)
        return out

    for c in range(D_ATTN // MXU_DIM):
        cols = slice(c * MXU_DIM, (c + 1) * MXU_DIM)
        for s, r in enumerate(norm_rope(proj[:, cols], seg_ref[...], gqk_ref[:, cols])):
            q_ref[:, c * MXU_DIM + s * LANES:c * MXU_DIM + (s + 1) * LANES] = r.astype(BF16)

    (k01,) = norm_rope(proj[:, D_ATTN:D_ATTN + D_KV], seg_ref[0:LANES, 0:LANES],
                       gqk_ref[:, D_ATTN:D_ATTN + D_KV])
    unit_lane = (lane == HEAD_DIM).astype(F32)
    k_ref[:, 0:LANES] = jnp.where(low_half, k01, unit_lane).astype(BF16)
    k_ref[:, LANES:] = jnp.where(low_half, pltpu.roll(k01, HEAD_DIM, 1), unit_lane).astype(BF16)
    v01 = proj[:, D_ATTN + D_KV:D_ATTN + 2 * D_KV]
    v_ref[:, 0:LANES] = jnp.where(low_half, v01, 1.0).astype(BF16)
    v_ref[:, LANES:] = jnp.where(low_half, pltpu.roll(v01, HEAD_DIM, 1), 1.0).astype(BF16)
    xbr_ref[...] = proj[:, D_ATTN + 2 * D_KV:D_ATTN + 2 * D_KV + D_LRU].astype(BF16)
    ybr_ref[...] = proj[:, D_ATTN + 2 * D_KV + D_LRU:].astype(BF16)


def _proj_call(x2d, g_mix, w_in, gqk, seg, cos_t, sin_t, seq_len):
    n_tok = x2d.shape[0]
    tiles_per_seq = seq_len // TM_PROJ
    const = lambda i: (0, 0)
    tok = lambda i: (i, 0)
    pos = lambda i: (i % tiles_per_seq, 0)
    return pl.pallas_call(
        _proj_kernel,
        grid=(n_tok // TM_PROJ,),
        in_specs=[
            pl.BlockSpec((TM_PROJ, D_MODEL), tok),
            pl.BlockSpec((1, D_MODEL), const),
            pl.BlockSpec((D_MODEL, D_PROJ), const),
            pl.BlockSpec((1, D_QK), const),
            pl.BlockSpec((MXU_DIM, MXU_DIM), const),
            pl.BlockSpec((TM_PROJ, LANES), pos),
            pl.BlockSpec((TM_PROJ, LANES), pos),
        ],
        out_specs=[
            pl.BlockSpec((TM_PROJ, D_ATTN), tok),
            pl.BlockSpec((TM_PROJ, D_KDUP), tok),
            pl.BlockSpec((TM_PROJ, D_KDUP), tok),
            pl.BlockSpec((TM_PROJ, D_LRU), tok),
            pl.BlockSpec((TM_PROJ, D_LRU), tok),
        ],
        out_shape=[
            jax.ShapeDtypeStruct((n_tok, D_ATTN), BF16),
            jax.ShapeDtypeStruct((n_tok, D_KDUP), BF16),
            jax.ShapeDtypeStruct((n_tok, D_KDUP), BF16),
            jax.ShapeDtypeStruct((n_tok, D_LRU), BF16),
            jax.ShapeDtypeStruct((n_tok, D_LRU), BF16),
        ],
        compiler_params=pltpu.CompilerParams(vmem_limit_bytes=VMEM_LIMIT),
        name="proj",
    )(x2d, g_mix, w_in, gqk, seg, cos_t, sin_t)


def _attn_kernel(flag_ref, q_ref, k_ref, v_ref, shift_ref, o_ref, qs_ref, acc_ref, m_ref, *, n_kv):
    j = pl.program_id(3)
    bounded = flag_ref[0] != 0
    tq, tk_step = q_ref.shape[0], k_ref.shape[0]
    ts = min(tq, TQ)
    rows_s = Q_PER_KV * ts
    lane = lax.broadcasted_iota(jnp.int32, (ts, LANES), 1)
    low_half = lane < HEAD_DIM
    chunks = [slice(c * TK_CHUNK, (c + 1) * TK_CHUNK) for c in range(tk_step // TK_CHUNK)]

    def stack_queries(sub):
        for hh in range(Q_PER_KV):
            pair = q_ref[sub * ts:(sub + 1) * ts, (hh // 2) * LANES:(hh // 2 + 1) * LANES].astype(F32)
            if hh % 2:
                pair = pltpu.roll(pair, HEAD_DIM, 1)
            base = sub * rows_s + hh * ts
            qs_ref[base:base + ts, :] = jnp.where(low_half, pair, shift_ref[...]).astype(BF16)

    def scores(sub, keys):
        return lax.dot_general(qs_ref[sub * rows_s:(sub + 1) * rows_s, :], k_ref[keys, :],
                               (((1,), (1,)), ((), ())),
                               preferred_element_type=F32)

    def bounded_chunks(sub, first_assigns):
        acc_rows = slice(sub * rows_s, (sub + 1) * rows_s)
        for c, keys in enumerate(chunks):
            p = jnp.exp2(scores(sub, keys)).astype(BF16)
            part = jnp.dot(p, v_ref[keys, :], preferred_element_type=F32)
            if first_assigns and c == 0:
                acc_ref[acc_rows, :] = part
            else:
                acc_ref[acc_rows, :] += part

    def online_chunks(sub):
        acc_rows = slice(sub * rows_s, (sub + 1) * rows_s)
        m = m_ref[acc_rows, :]
        acc = acc_ref[acc_rows, :]
        for keys in chunks:
            s = scores(sub, keys)
            m_new = jnp.maximum(m, jnp.max(s, axis=1, keepdims=True))
            alpha = jnp.exp2(m - m_new)
            p = jnp.exp2(s - jnp.tile(m_new, (1, TK_CHUNK // LANES))).astype(BF16)
            acc = alpha * acc + jnp.dot(p, v_ref[keys, :], preferred_element_type=F32)
            m = m_new
        m_ref[acc_rows, :] = m
        acc_ref[acc_rows, :] = acc

    def init_online():
        acc_ref[...] = jnp.zeros(acc_ref.shape, F32)
        m_ref[...] = jnp.full(m_ref.shape, NEG_BIG, F32)

    def finalize(sub):
        a = acc_ref[sub * rows_s:(sub + 1) * rows_s, :]
        swapped = pltpu.roll(a, HEAD_DIM, 1)
        for pr in range(Q_PER_KV // 2):
            ev = slice((2 * pr) * ts, (2 * pr + 1) * ts)
            od = slice((2 * pr + 1) * ts, (2 * pr + 2) * ts)
            num = jnp.where(low_half, a[ev], swapped[od])
            den = jnp.where(low_half, swapped[ev], a[od])
            o_ref[sub * ts:(sub + 1) * ts, pr * LANES:(pr + 1) * LANES] = (num / den).astype(BF16)

    subs = range(tq // ts)
    if n_kv == 1:
        @pl.when(bounded)
        def _():
            for sub in subs:
                stack_queries(sub)
                bounded_chunks(sub, first_assigns=True)
                finalize(sub)

        @pl.when(jnp.logical_not(bounded))
        def _():
            init_online()
            for sub in subs:
                stack_queries(sub)
                online_chunks(sub)
                finalize(sub)
    else:
        @pl.when(j == 0)
        def _():
            for sub in subs:
                stack_queries(sub)
            init_online()

        @pl.when(bounded)
        def _():
            for sub in subs:
                bounded_chunks(sub, first_assigns=False)

        @pl.when(jnp.logical_not(bounded))
        def _():
            for sub in subs:
                online_chunks(sub)

        @pl.when(j == n_kv - 1)
        def _():
            for sub in subs:
                finalize(sub)


def _attention(q, ka, va, score_bound):
    b, s, _ = q.shape
    tk_step = min(s, TK_STEP)
    n_kv = s // tk_step
    tq = min(s, TQ_ONE_STEP)
    n_q = s // tq
    rows = Q_PER_KV * tq
    use_bounded = 2.0 * score_bound <= MAX_SHIFTED_RANGE
    flag = use_bounded.astype(jnp.int32).reshape(1)
    shift = jnp.zeros((1, LANES), F32).at[0, HEAD_DIM].set(jnp.where(use_bounded, -score_bound, 0.0))
    return pl.pallas_call(
        functools.partial(_attn_kernel, n_kv=n_kv),
        grid=(b, N_KV_HEADS, n_q, n_kv),
        in_specs=[
            pl.BlockSpec(memory_space=pltpu.SMEM),
            pl.BlockSpec((None, tq, Q_PER_KV * HEAD_DIM), lambda bi, g, i, j: (bi, i, g)),
            pl.BlockSpec((None, tk_step, LANES), lambda bi, g, i, j: (bi, j, g)),
            pl.BlockSpec((None, tk_step, LANES), lambda bi, g, i, j: (bi, j, g)),
            pl.BlockSpec((1, LANES), lambda bi, g, i, j: (0, 0)),
        ],
        out_specs=pl.BlockSpec((None, tq, Q_PER_KV * HEAD_DIM), lambda bi, g, i, j: (bi, i, g)),
        out_shape=jax.ShapeDtypeStruct((b, s, D_ATTN), BF16),
        scratch_shapes=[
            pltpu.VMEM((rows, LANES), BF16),
            pltpu.VMEM((rows, LANES), F32),
            pltpu.VMEM((rows, LANES), F32),
        ],
        compiler_params=pltpu.CompilerParams(vmem_limit_bytes=VMEM_LIMIT),
        name="attn",
    )(flag, q, ka, va, shift)


def _lru_conv(x_ref, prev_ref, next_ref, cw_ref, cb_ref, has_prev, has_next):
    t_rows = x_ref.shape[0]
    x = x_ref[...].astype(F32)
    prev = prev_ref[...].astype(F32) * has_prev
    nxt = next_ref[...].astype(F32) * has_next
    row8 = lax.broadcasted_iota(jnp.int32, (SUBLANES, D_LRU), 0)

    def shifted(k):
        if k == 0:
            return x
        rolled = pltpu.roll(x, (-k) % t_rows, 0)
        if k < 0:
            head = jnp.where(row8 < -k, pltpu.roll(prev, -k, 0), rolled[0:SUBLANES])
            return jnp.concatenate([head, rolled[SUBLANES:]], axis=0)
        tail = jnp.where(row8 >= SUBLANES - k, pltpu.roll(nxt, SUBLANES - k, 0), rolled[t_rows - SUBLANES:])
        return jnp.concatenate([rolled[:t_rows - SUBLANES], tail], axis=0)

    xc = cb_ref[...]
    for w in range(CONV_W):
        xc = xc + cw_ref[w:w + 1, :] * shifted(w - 2)
    return xc


def _lru_coeffs(xc, wa_ref, ba_ref, wx_ref, bx_ref, lam_ref, a_ref, u_ref):
    t_rows = xc.shape[0]
    xcb = xc.astype(BF16)
    half = D_LRU // 2
    r_parts, i_parts = [], []
    for c in range(2):
        xh = xcb[:, c * half:(c + 1) * half]
        r_parts.append(jnp.dot(xh, wa_ref[c], preferred_element_type=F32))
        i_parts.append(jnp.dot(xh, wx_ref[c], preferred_element_type=F32))
    t_r = jnp.tanh(jnp.concatenate(r_parts, axis=1) + ba_ref[...])
    t_i = jnp.tanh(jnp.concatenate(i_parts, axis=1) + bx_ref[...])
    lam = lam_ref[...]
    neg_softplus = -(jnp.maximum(-lam, 0.0) + jnp.log1p(jnp.exp(-jnp.abs(lam))))
    log_a = (0.5 * LRU_C * neg_softplus) * (1.0 + t_r)
    a = jnp.exp(log_a)
    th = jnp.tanh(log_a)
    u = jnp.exp2(0.5 * jnp.log2(-0.5 * th / (1.0 - th))) * ((1.0 + t_i) * xc)
    for lt in range(LRU_SLABS):
        a_ref[lt, 0:t_rows, :] = a[:, lt * LANES:(lt + 1) * LANES]
        u_ref[lt, 0:t_rows, :] = u[:, lt * LANES:(lt + 1) * LANES]
        a_ref[lt, t_rows:, :] = jnp.ones((LRU_PAD_ROWS - t_rows, LANES), F32)
        u_ref[lt, t_rows:, :] = jnp.zeros((LRU_PAD_ROWS - t_rows, LANES), F32)


def _sublane_shift(x, k, fill, row, reverse):
    if reverse:
        return jnp.where(row < SUBLANES - k, pltpu.roll(x, SUBLANES - k, 0), fill)
    return jnp.where(row >= k, pltpu.roll(x, k, 0), fill)


def _scan_tile(a_ref, u_ref, h_ref, carry_ref, reverse):
    row = lax.broadcasted_iota(jnp.int32, (SUBLANES, LANES), 0)

    n_blocks = LRU_CHUNK // LRU_UNROLL

    def block_rows(blk):
        if reverse:
            base = (n_blocks - 1 - blk) * LRU_UNROLL
            offsets = range(LRU_UNROLL - 1, -1, -1)
        else:
            base = blk * LRU_UNROLL
            offsets = range(LRU_UNROLL)
        return [pl.ds(base + k, SUBLANES, stride=LRU_CHUNK) for k in offsets]

    def reduce_block(blk, au):
        acc_a, acc_u = list(au[0]), list(au[1])
        for rows in block_rows(blk):
            for lt in range(LRU_SLABS):
                a_t = a_ref[lt, rows, :]
                acc_u[lt] = a_t * acc_u[lt] + u_ref[lt, rows, :]
                acc_a[lt] = a_t * acc_a[lt]
        return tuple(acc_a), tuple(acc_u)

    ones = tuple(jnp.ones((SUBLANES, LANES), F32) for _ in range(LRU_SLABS))
    zeros = tuple(jnp.zeros((SUBLANES, LANES), F32) for _ in range(LRU_SLABS))
    tot_a, tot_u = lax.fori_loop(0, n_blocks, reduce_block, (ones, zeros))

    h_in = []
    for lt in range(LRU_SLABS):
        a, u = tot_a[lt], tot_u[lt]
        for k in (1, 2, 4):
            u = u + a * _sublane_shift(u, k, 0.0, row, reverse)
            a = a * _sublane_shift(a, k, 1.0, row, reverse)
        carry = carry_ref[lt]
        h_out = u + a * carry
        h_in.append(_sublane_shift(h_out, 1, carry, row, reverse))
        last = h_out[0:1, :] if reverse else h_out[SUBLANES - 1:SUBLANES, :]
        carry_ref[lt] = jnp.broadcast_to(last, (SUBLANES, LANES))

    def scan_block(blk, hs):
        hs = list(hs)
        for rows in block_rows(blk):
            for lt in range(LRU_SLABS):
                hs[lt] = a_ref[lt, rows, :] * hs[lt] + u_ref[lt, rows, :]
                h_ref[lt, rows, :] = hs[lt]
        return tuple(hs)

    lax.fori_loop(0, n_blocks, scan_block, tuple(h_in))


def _lru_fwd_kernel(x_ref, prev_ref, next_ref, cw_ref, cb_ref, wa_ref, ba_ref, wx_ref, bx_ref, lam_ref,
                    hf_ref, xc_ref, a_ref, u_ref, h_ref, carry_ref, *, n_tiles):
    i = pl.program_id(1)

    @pl.when(i == 0)
    def _():
        carry_ref[...] = jnp.zeros(carry_ref.shape, F32)

    has_prev = (i > 0).astype(F32)
    has_next = (i < n_tiles - 1).astype(F32)
    xc = _lru_conv(x_ref, prev_ref, next_ref, cw_ref, cb_ref, has_prev, has_next)
    xc_ref[...] = xc
    _lru_coeffs(xc, wa_ref, ba_ref, wx_ref, bx_ref, lam_ref, a_ref, u_ref)
    _scan_tile(a_ref, u_ref, h_ref, carry_ref, reverse=False)
    for lt in range(LRU_SLABS):
        hf_ref[:, lt * LANES:(lt + 1) * LANES] = h_ref[lt, 0:T_LRU, :]


def _lru_bwd_kernel(xc_ref, y_ref, hf_ref, wa_ref, ba_ref, wx_ref, bx_ref, lam_ref,
                    o_ref, a_ref, u_ref, h_ref, carry_ref):
    @pl.when(pl.program_id(1) == 0)
    def _():
        carry_ref[...] = jnp.zeros(carry_ref.shape, F32)

    _lru_coeffs(xc_ref[...], wa_ref, ba_ref, wx_ref, bx_ref, lam_ref, a_ref, u_ref)
    _scan_tile(a_ref, u_ref, h_ref, carry_ref, reverse=True)
    for lt in range(LRU_SLABS):
        cols = slice(lt * LANES, (lt + 1) * LANES)
        y = y_ref[:, cols].astype(F32)
        inner = y * (GELU_C + (GELU_C * 0.044715) * (y * y))
        gelu = y * (0.5 + 0.5 * jnp.tanh(inner))
        o_ref[:, cols] = ((hf_ref[:, cols] + h_ref[lt, 0:T_LRU, :]) * gelu).astype(BF16)


def _lru_call(xbr, ybr, conv_w, conv_b, wa, ba, wx, bx, lam):
    b, s, _ = xbr.shape
    n_tiles = s // T_LRU
    blocks_per_tile = T_LRU // SUBLANES
    n_blocks = n_tiles * blocks_per_tile
    half = D_LRU // 2
    slab = pltpu.VMEM((LRU_SLABS, LRU_PAD_ROWS, LANES), F32)
    scratch = [slab, slab, slab, pltpu.VMEM((LRU_SLABS, SUBLANES, LANES), F32)]
    params = pltpu.CompilerParams(vmem_limit_bytes=VMEM_LIMIT)
    c2 = lambda bi, i: (0, 0)
    c3 = lambda bi, i: (0, 0, 0)
    row_spec = pl.BlockSpec((1, D_LRU), c2)
    gate_specs = [pl.BlockSpec((2, half, half), c3), row_spec, pl.BlockSpec((2, half, half), c3),
                  row_spec, row_spec]

    fwd_tile = pl.BlockSpec((None, T_LRU, D_LRU), lambda bi, i: (bi, i, 0))
    prev = lambda bi, i: (bi, jnp.maximum(i * blocks_per_tile - 1, 0), 0)
    nxt = lambda bi, i: (bi, jnp.minimum((i + 1) * blocks_per_tile, n_blocks - 1), 0)
    h_fwd, xc = pl.pallas_call(
        functools.partial(_lru_fwd_kernel, n_tiles=n_tiles),
        grid=(b, n_tiles),
        in_specs=[fwd_tile, pl.BlockSpec((None, SUBLANES, D_LRU), prev),
                  pl.BlockSpec((None, SUBLANES, D_LRU), nxt),
                  pl.BlockSpec((CONV_W, D_LRU), c2), row_spec] + gate_specs,
        out_specs=[fwd_tile, fwd_tile],
        out_shape=[jax.ShapeDtypeStruct((b, s, D_LRU), F32), jax.ShapeDtypeStruct((b, s, D_LRU), F32)],
        scratch_shapes=scratch,
        compiler_params=params,
        name="lru_fwd",
    )(xbr, xbr, xbr, conv_w, conv_b, wa[0], ba[0:1], wx[0], bx[0:1], lam[0:1])

    bwd_tile = pl.BlockSpec((None, T_LRU, D_LRU), lambda bi, i: (bi, n_tiles - 1 - i, 0))
    return pl.pallas_call(
        _lru_bwd_kernel,
        grid=(b, n_tiles),
        in_specs=[bwd_tile, bwd_tile, bwd_tile] + gate_specs,
        out_specs=bwd_tile,
        out_shape=jax.ShapeDtypeStruct((b, s, D_LRU), BF16),
        scratch_shapes=scratch,
        compiler_params=params,
        name="lru_bwd",
    )(xc, ybr, h_fwd, wa[1], ba[1:2], wx[1], bx[1:2], lam[1:2])


def _mlp_kernel(x_ref, attn_ref, rec_ref, wo_ref, g2_ref, wup_ref, wdn_ref, gf_ref, o_ref):
    x1 = (x_ref[...]
          + jnp.dot(attn_ref[...], wo_ref[0:D_ATTN, :], preferred_element_type=F32)
          + jnp.dot(rec_ref[...], wo_ref[D_ATTN:, :], preferred_element_type=F32))
    h = (x1 * _rms_scale(x1) * g2_ref[...]).astype(BF16)
    mlp = None
    for c in range(D_FF // FF_CHUNK):
        up = jnp.dot(h, wup_ref[:, c * FF_CHUNK:(c + 1) * FF_CHUNK], preferred_element_type=F32)
        act = jnp.square(jnp.maximum(up, 0.0)).astype(BF16)
        down = jnp.dot(act, wdn_ref[c * FF_CHUNK:(c + 1) * FF_CHUNK, :], preferred_element_type=F32)
        mlp = down if mlp is None else mlp + down
    x2 = x1 + mlp
    o_ref[...] = x2 * _rms_scale(x2) * gf_ref[...]


def _mlp_call(x2d, attn2d, rec2d, w_out, g_mlp, w_up, w_down, g_final):
    n_tok = x2d.shape[0]
    const = lambda i: (0, 0)
    tok = lambda i: (i, 0)
    resident = pl.Buffered(1)
    return pl.pallas_call(
        _mlp_kernel,
        grid=(n_tok // TM_MLP,),
        in_specs=[
            pl.BlockSpec((TM_MLP, D_MODEL), tok),
            pl.BlockSpec((TM_MLP, D_ATTN), tok),
            pl.BlockSpec((TM_MLP, D_LRU), tok),
            pl.BlockSpec((D_MODEL, D_MODEL), const, pipeline_mode=resident),
            pl.BlockSpec((1, D_MODEL), const),
            pl.BlockSpec((D_MODEL, D_FF), const, pipeline_mode=resident),
            pl.BlockSpec((D_FF, D_MODEL), const, pipeline_mode=resident),
            pl.BlockSpec((1, D_MODEL), const),
        ],
        out_specs=pl.BlockSpec((TM_MLP, D_MODEL), tok),
        out_shape=jax.ShapeDtypeStruct((n_tok, D_MODEL), F32),
        compiler_params=pltpu.CompilerParams(vmem_limit_bytes=VMEM_LIMIT),
        name="mlp",
    )(x2d, attn2d, rec2d, w_out, g_mlp, w_up, w_down, g_final)


def _rope_tables(seq_len):
    rows = seq_len // GRID_W
    row_ids = jnp.repeat(jnp.arange(rows), GRID_W).astype(F32)
    col_ids = jnp.tile(jnp.arange(GRID_W), rows).astype(F32)
    inv_freq = ROPE_THETA ** (-jnp.arange(0, ROPE_HALF, 2, dtype=F32) / ROPE_HALF)
    ang_r = row_ids[:, None] * inv_freq
    ang_c = col_ids[:, None] * inv_freq
    cos = jnp.concatenate([jnp.cos(ang_r)] * 2 + [jnp.cos(ang_c)] * 2, axis=-1)
    sin = jnp.concatenate([-jnp.sin(ang_r), jnp.sin(ang_r), -jnp.sin(ang_c), jnp.sin(ang_c)], axis=-1)
    return jnp.tile(cos, (1, LANES // HEAD_DIM)), jnp.tile(sin, (1, LANES // HEAD_DIM))


def _block_diag_halves(w):
    per_half = N_LRU_BLOCKS // 2
    halves = [jax.scipy.linalg.block_diag(*[w[c * per_half + n] for n in range(per_half)])
              for c in range(2)]
    return jnp.stack(halves).astype(BF16)


def _layer(x, params, seq_tables):
    b, s, _ = x.shape
    x2d = x.reshape(b * s, D_MODEL)
    q, kd, va, xbr, ybr = _proj_call(x2d, params["g_mix"], params["w_in"], params["gqk"],
                                     params["seg"], seq_tables[0], seq_tables[1], s)
    attn = _attention(q.reshape(b, s, D_ATTN), kd.reshape(b, s, D_KDUP), va.reshape(b, s, D_KDUP),
                      params["score_bound"])
    rec = _lru_call(xbr.reshape(b, s, D_LRU), ybr.reshape(b, s, D_LRU), params["conv_w"],
                    params["conv_b"], params["wa"], params["ba"], params["wx"], params["bx"],
                    params["lam"])
    y = _mlp_call(x2d, attn.reshape(b * s, D_ATTN), rec.reshape(b * s, D_LRU), params["w_out"],
                  params["g_mlp"], params["w_up"], params["w_down"], params["g_final"])
    return y.reshape(b, s, D_MODEL)


def kernel(x_prompt, x_sample, norm_mix_g, w_in, q_norm_g, k_norm_g, conv_w, conv_b, lru_wa, lru_ba,
           lru_wx, lru_bx, lru_lambda, w_out, norm_mlp_g, w_up, w_down, norm_final_g):
    l = 0
    q_gain = q_norm_g[l] * (HEAD_DIM ** -0.5 * math.log2(math.e))
    gqk = jnp.concatenate([jnp.tile(q_gain, N_Q_HEADS), jnp.tile(k_norm_g[l], N_KV_HEADS)])
    seg = jax.scipy.linalg.block_diag(*[jnp.ones((HEAD_DIM, HEAD_DIM), F32)] * (MXU_DIM // HEAD_DIM))
    score_bound = (HEAD_DIM * SCORE_BOUND_SLACK) * jnp.max(jnp.abs(q_gain)) * jnp.max(jnp.abs(k_norm_g[l]))
    params = {
        "score_bound": score_bound,
        "g_mix": norm_mix_g[l][None, :],
        "w_in": w_in[l].astype(BF16),
        "gqk": gqk[None, :],
        "seg": seg.astype(BF16),
        "conv_w": conv_w[l],
        "conv_b": conv_b[l][None, :],
        "wa": jnp.stack([_block_diag_halves(0.5 * lru_wa[l, d]) for d in range(2)]),
        "ba": 0.5 * lru_ba[l],
        "wx": jnp.stack([_block_diag_halves(0.5 * lru_wx[l, d]) for d in range(2)]),
        "bx": 0.5 * lru_bx[l],
        "lam": lru_lambda[l],
        "w_out": w_out[l].astype(BF16),
        "g_mlp": norm_mlp_g[l][None, :],
        "w_up": w_up[l].astype(BF16),
        "w_down": w_down[l].astype(BF16),
        "g_final": norm_final_g[None, :],
    }
    outs = []
    for x in (x_prompt, x_sample):
        outs.append(_layer(x, params, _rope_tables(x.shape[1])))
    return tuple(outs)
```
